```python
import jax, jax.numpy as jnp
from jax import lax
import numpy as np

D_MODEL = 2048
BATCH = 4
SEQ = 4096
DEPTH = 1
DEC_BATCH = 1
DEC_SEQ = 16384
PAST_LEN = 128

N_MEM = 256
D_INNER = 2 * D_MODEL
D_SSD = D_INNER // 2
SSD_HEADDIM = 64
SSD_HEADS = D_SSD // SSD_HEADDIM
SSD_GROUPS = 4
SSD_HPG = SSD_HEADS // SSD_GROUPS
SSD_STATE = 128
SSD_CHUNK = 128
CONV_WIDTH = 5
D_CONV = D_SSD + 2 * SSD_GROUPS * SSD_STATE
D_GLA_V = D_INNER - D_SSD
D_GLA_K = D_GLA_V // 2
GLA_HEADS = 4
GLA_HEAD_K = D_GLA_K // GLA_HEADS
GLA_HEAD_V = D_GLA_V // GLA_HEADS
GLA_RANK = 16
GLA_NORMALIZER = 16.0
GLA_CHUNK = 64
XATTN_HEADS = 4
XATTN_HEAD_DIM = D_MODEL // XATTN_HEADS
D_FF = ((8 * D_MODEL // 3 + 255) // 256) * 256
EPS = 1e-6
IN_SPLITS = (D_SSD,
             D_CONV,
             2 * SSD_HEADS,
             D_GLA_K,
             D_GLA_K,
             D_GLA_V,
             2 * GLA_RANK,
             D_GLA_V)
D_IN_PROJ = D_SSD + D_CONV + 2 * SSD_HEADS + 2 * D_GLA_K + D_GLA_V + 2 * GLA_RANK + D_GLA_V

kernel_name = "hymba_ssd_gla_macaron_memory_encoder"


def rmsnorm(x, w):
    xf = x.astype(jnp.float32)
    y = xf * lax.rsqrt(jnp.mean(xf * xf, axis=-1, keepdims=True) + EPS)
    return (y * w.astype(jnp.float32)).astype(x.dtype)


def rev(t):
    return jnp.flip(t, axis=1)


def swiglu(x, w1, w3, w2):
    return (jax.nn.silu(x @ w1) * (x @ w3)) @ w2


def centred_depthwise_conv(x, w, b):
    pad = CONV_WIDTH // 2
    y = lax.conv_general_dilated(x, w[:, None, :].astype(x.dtype), window_strides=(1,),
                                 padding=[(pad, pad)], dimension_numbers=('NWC', 'WIO', 'NWC'),
                                 feature_group_count=x.shape[-1])
    return y + b.astype(x.dtype)


def ssd_causal(x, dt, A, bm, cm):
    b, L, H, P = x.shape
    Q, G, R, N = SSD_CHUNK, SSD_GROUPS, SSD_HPG, SSD_STATE
    nc = L // Q
    f32 = jnp.float32
    x = x.astype(f32).reshape(b, nc, Q, G, R, P)
    dt = dt.astype(f32).reshape(b, nc, Q, G, R)
    bm = bm.astype(f32).reshape(b, nc, Q, G, N)
    cm = cm.astype(f32).reshape(b, nc, Q, G, N)
    acum = jnp.cumsum(dt * A.astype(f32).reshape(G, R), axis=2)
    causal = jnp.tril(jnp.ones((Q, Q), dtype=bool))
    seg = acum[:, :, :, None] - acum[:, :, None]
    decay = jnp.exp(jnp.where(causal[:, :, None, None], seg, -jnp.inf))
    cb = jnp.einsum('bcign,bcjgn->bcijg', cm, bm)
    xdt = x * dt[..., None]
    y_diag = jnp.einsum('bcijgr,bcjgrp->bcigrp', cb[..., None] * decay, xdt)
    x_end = xdt * jnp.exp(acum[:, :, -1:] - acum)[..., None]
    chunk_decay = jnp.exp(acum[:, :, -1])

    def step(S, inp):
        b_c, x_c, c_c, ea_c, dec_c = inp
        y = jnp.einsum('bign,bgrnp->bigrp', c_c, S) * ea_c[..., None]
        S = S * dec_c[..., None, None] + jnp.einsum('bjgn,bjgrp->bgrnp', b_c, x_c)
        return S, y

    S0 = jnp.zeros((b, G, R, N, P), f32)
    xs = (jnp.moveaxis(bm, 1, 0), jnp.moveaxis(x_end, 1, 0), jnp.moveaxis(cm, 1, 0),
          jnp.moveaxis(jnp.exp(acum), 1, 0), jnp.moveaxis(chunk_decay, 1, 0))
    _, y_off = lax.scan(step, S0, xs)
    y = y_diag + jnp.moveaxis(y_off, 0, 1)
    return y.reshape(b, L, H, P)


def gla_causal(q, k, v, gk):
    b, L, H, K = q.shape
    V = v.shape[-1]
    Q = GLA_CHUNK
    nc = L // Q
    f32 = jnp.float32
    q = q.astype(f32).reshape(b, nc, Q, H, K)
    k = k.astype(f32).reshape(b, nc, Q, H, K)
    v = v.astype(f32).reshape(b, nc, Q, H, V)
    G = jnp.cumsum(gk.astype(f32).reshape(b, nc, Q, H, K), axis=2)
    g_mid = G[:, :, Q // 2 - 1:Q // 2]
    a = jnp.einsum('bcihk,bcjhk->bchij', q * jnp.exp(G - g_mid), k * jnp.exp(g_mid - G))
    causal = jnp.tril(jnp.ones((Q, Q), dtype=bool))
    a = jnp.where(causal, a, 0.0)
    o_intra = jnp.einsum('bchij,bcjhv->bcihv', a, v)
    q_in = q * jnp.exp(G)
    k_end = k * jnp.exp(G[:, :, -1:] - G)
    dec = jnp.exp(G[:, :, -1])

    def step(S, inp):
        qc, kc, vc, dc = inp
        o = jnp.einsum('bihk,bhkv->bihv', qc, S)
        S = S * dc[..., None] + jnp.einsum('bjhk,bjhv->bhkv', kc, vc)
        return S, o

    S0 = jnp.zeros((b, H, K, V), f32)
    xs = (jnp.moveaxis(q_in, 1, 0), jnp.moveaxis(k_end, 1, 0), jnp.moveaxis(v, 1, 0),
          jnp.moveaxis(dec, 1, 0))
    _, o_inter = lax.scan(step, S0, xs)
    o = o_intra + jnp.moveaxis(o_inter, 0, 1)
    return o.reshape(b, L, H, V)


def hybrid_mixer(h, w_in, conv_w, conv_b, dt_bias_fwd, dt_bias_bwd, a_log_fwd, a_log_bwd,
                 d_skip, ssd_norm, gla_gate_w_fwd, gla_gate_b_fwd, gla_gate_w_bwd, gla_gate_b_bwd,
                 gla_norm, w_out):
    b, L, _ = h.shape
    f32 = jnp.float32
    split_points = np.cumsum(IN_SPLITS)[:-1].tolist()
    z, xbc, dt_raw, q, k, v, g_low, g_out = jnp.split(h @ w_in, split_points, axis=-1)

    xbc = jax.nn.silu(centred_depthwise_conv(xbc, conv_w, conv_b))
    xs, bm, cm = jnp.split(xbc, [D_SSD, D_SSD + SSD_GROUPS * SSD_STATE], axis=-1)
    xs = xs.reshape(b, L, SSD_HEADS, SSD_HEADDIM)
    bm = bm.reshape(b, L, SSD_GROUPS, SSD_STATE)
    cm = cm.reshape(b, L, SSD_GROUPS, SSD_STATE)
    dt_raw = dt_raw.astype(f32)
    dt_f = jax.nn.softplus(dt_raw[..., :SSD_HEADS] + dt_bias_fwd.astype(f32))
    dt_b = jax.nn.softplus(dt_raw[..., SSD_HEADS:] + dt_bias_bwd.astype(f32))
    y = (ssd_causal(xs, dt_f, -jnp.exp(a_log_fwd.astype(f32)), bm, cm)
         + rev(ssd_causal(rev(xs), rev(dt_b), -jnp.exp(a_log_bwd.astype(f32)), rev(bm), rev(cm))))
    y = y + d_skip.astype(f32)[:, None] * xs.astype(f32)
    y = y.reshape(b, L, D_SSD) * jax.nn.silu(z.astype(f32))
    y = rmsnorm(y.reshape(b, L, SSD_GROUPS, D_SSD // SSD_GROUPS),
                ssd_norm.reshape(SSD_GROUPS, D_SSD // SSD_GROUPS)).reshape(b, L, D_SSD)

    q = q.reshape(b, L, GLA_HEADS, GLA_HEAD_K) * GLA_HEAD_K ** -0.5
    k = k.reshape(b, L, GLA_HEADS, GLA_HEAD_K)
    v = v.reshape(b, L, GLA_HEADS, GLA_HEAD_V)
    low_f, low_b = jnp.split(g_low, 2, axis=-1)
    gk_f = jax.nn.log_sigmoid((low_f @ gla_gate_w_fwd + gla_gate_b_fwd).astype(f32)) / GLA_NORMALIZER
    gk_b = jax.nn.log_sigmoid((low_b @ gla_gate_w_bwd + gla_gate_b_bwd).astype(f32)) / GLA_NORMALIZER
    gk_f = gk_f.reshape(b, L, GLA_HEADS, GLA_HEAD_K)
    gk_b = gk_b.reshape(b, L, GLA_HEADS, GLA_HEAD_K)
    o = gla_causal(q, k, v, gk_f) + rev(gla_causal(rev(q), rev(k), rev(v), rev(gk_b)))
    o = rmsnorm(o, gla_norm).reshape(b, L, D_GLA_V) * jax.nn.silu(g_out.astype(f32))

    mixed = jnp.concatenate([y, o], axis=-1).astype(h.dtype)
    return mixed @ w_out


def memory_cross_attention(h, mem, mem_norm, w_cq, w_ckv, w_co):
    b, L, _ = h.shape
    m = rmsnorm(mem, mem_norm)
    q = (h @ w_cq).reshape(b, L, XATTN_HEADS, XATTN_HEAD_DIM)
    k, v = jnp.split(m @ w_ckv, 2, axis=-1)
    k = k.reshape(b, N_MEM, XATTN_HEADS, XATTN_HEAD_DIM)
    v = v.reshape(b, N_MEM, XATTN_HEADS, XATTN_HEAD_DIM)
    s = jnp.einsum('blhd,bmhd->bhlm', q, k).astype(jnp.float32) * XATTN_HEAD_DIM ** -0.5
    p = jax.nn.softmax(s, axis=-1).astype(v.dtype)
    o = jnp.einsum('bhlm,bmhd->blhd', p, v).reshape(b, L, D_MODEL)
    return o @ w_co


def encoder_layer(x, mem, ffn1_norm, ffn1_w1, ffn1_w3, ffn1_w2,
                  mix_norm, w_in, conv_w, conv_b, dt_bias_fwd, dt_bias_bwd, a_log_fwd, a_log_bwd,
                  d_skip, ssd_norm, gla_gate_w_fwd, gla_gate_b_fwd, gla_gate_w_bwd, gla_gate_b_bwd,
                  gla_norm, w_out, xattn_norm, mem_norm, w_cq, w_ckv, w_co,
                  ffn2_norm, ffn2_w1, ffn2_w3, ffn2_w2):
    x = x + 0.5 * swiglu(rmsnorm(x, ffn1_norm), ffn1_w1, ffn1_w3, ffn1_w2)
    x = x + hybrid_mixer(rmsnorm(x, mix_norm), w_in, conv_w, conv_b, dt_bias_fwd, dt_bias_bwd,
                         a_log_fwd, a_log_bwd, d_skip, ssd_norm, gla_gate_w_fwd, gla_gate_b_fwd,
                         gla_gate_w_bwd, gla_gate_b_bwd, gla_norm, w_out)
    x = x + memory_cross_attention(rmsnorm(x, xattn_norm), mem, mem_norm, w_cq, w_ckv, w_co)
    x = x + 0.5 * swiglu(rmsnorm(x, ffn2_norm), ffn2_w1, ffn2_w3, ffn2_w2)
    return x


def trunk(x, mem, layer_stack, final_norm):
    for l in range(DEPTH):
        x = encoder_layer(x, mem, *[p[l] for p in layer_stack])
    return rmsnorm(x, final_norm)


def setup_inputs(seed: int = 0) -> dict:
    key = jax.random.key(seed)
    ks = iter(jax.random.split(key, 48))
    f32 = jnp.float32

    def nrm(shape, scale):
        return jax.random.normal(next(ks), shape, f32) * scale

    def gain(shape):
        return 1.0 + nrm(shape, 0.05)

    def dt_bias(shape):
        u = jax.random.uniform(next(ks), shape, f32, minval=float(np.log(1e-3)), maxval=float(np.log(1e-1)))
        dt = jnp.exp(u)
        return dt + jnp.log(-jnp.expm1(-dt))

    def a_log(shape):
        return jnp.log(jax.random.uniform(next(ks), shape, f32, minval=1.0, maxval=16.0))

    Lr = DEPTH
    return {
        'x_prompt': nrm((BATCH, SEQ, D_MODEL), 1.0),
        'x_sample': nrm((DEC_BATCH, DEC_SEQ, D_MODEL), 1.0),
        'mem_prompt': nrm((BATCH, N_MEM, D_MODEL), 1.0),
        'mem_sample': nrm((DEC_BATCH, N_MEM, D_MODEL), 1.0),
        'ffn1_norm': gain((Lr, D_MODEL)),
        'ffn1_w1': nrm((Lr, D_MODEL, D_FF), D_MODEL ** -0.5),
        'ffn1_w3': nrm((Lr, D_MODEL, D_FF), D_MODEL ** -0.5),
        'ffn1_w2': nrm((Lr, D_FF, D_MODEL), D_FF ** -0.5),
        'mix_norm': gain((Lr, D_MODEL)),
        'w_in': nrm((Lr, D_MODEL, D_IN_PROJ), D_MODEL ** -0.5),
        'conv_w': nrm((Lr, CONV_WIDTH, D_CONV), CONV_WIDTH ** -0.5),
        'conv_b': nrm((Lr, D_CONV), 0.02),
        'dt_bias_fwd': dt_bias((Lr, SSD_HEADS)),
        'dt_bias_bwd': dt_bias((Lr, SSD_HEADS)),
        'a_log_fwd': a_log((Lr, SSD_HEADS)),
        'a_log_bwd': a_log((Lr, SSD_HEADS)),
        'd_skip': gain((Lr, SSD_HEADS)),
        'ssd_norm': gain((Lr, D_SSD)),
        'gla_gate_w_fwd': nrm((Lr, GLA_RANK, D_GLA_K), GLA_RANK ** -0.5),
        'gla_gate_b_fwd': nrm((Lr, D_GLA_K), 0.1),
        'gla_gate_w_bwd': nrm((Lr, GLA_RANK, D_GLA_K), GLA_RANK ** -0.5),
        'gla_gate_b_bwd': nrm((Lr, D_GLA_K), 0.1),
        'gla_norm': gain((Lr, GLA_HEAD_V)),
        'w_out': nrm((Lr, D_INNER, D_MODEL), D_INNER ** -0.5),
        'xattn_norm': gain((Lr, D_MODEL)),
        'mem_norm': gain((Lr, D_MODEL)),
        'w_cq': nrm((Lr, D_MODEL, D_MODEL), D_MODEL ** -0.5),
        'w_ckv': nrm((Lr, D_MODEL, 2 * D_MODEL), D_MODEL ** -0.5),
        'w_co': nrm((Lr, D_MODEL, D_MODEL), D_MODEL ** -0.5),
        'ffn2_norm': gain((Lr, D_MODEL)),
        'ffn2_w1': nrm((Lr, D_MODEL, D_FF), D_MODEL ** -0.5),
        'ffn2_w3': nrm((Lr, D_MODEL, D_FF), D_MODEL ** -0.5),
        'ffn2_w2': nrm((Lr, D_FF, D_MODEL), D_FF ** -0.5),
        'final_norm': gain((D_MODEL,)),
    }


def reference(x_prompt, x_sample, mem_prompt, mem_sample,
              ffn1_norm, ffn1_w1, ffn1_w3, ffn1_w2,
              mix_norm, w_in, conv_w, conv_b, dt_bias_fwd, dt_bias_bwd, a_log_fwd, a_log_bwd,
              d_skip, ssd_norm, gla_gate_w_fwd, gla_gate_b_fwd, gla_gate_w_bwd, gla_gate_b_bwd,
              gla_norm, w_out, xattn_norm, mem_norm, w_cq, w_ckv, w_co,
              ffn2_norm, ffn2_w1, ffn2_w3, ffn2_w2, final_norm):
    layer_stack = (ffn1_norm, ffn1_w1, ffn1_w3, ffn1_w2,
                   mix_norm, w_in, conv_w, conv_b, dt_bias_fwd, dt_bias_bwd, a_log_fwd, a_log_bwd,
                   d_skip, ssd_norm, gla_gate_w_fwd, gla_gate_b_fwd, gla_gate_w_bwd, gla_gate_b_bwd,
                   gla_norm, w_out, xattn_norm, mem_norm, w_cq, w_ckv, w_co,
                   ffn2_norm, ffn2_w1, ffn2_w3, ffn2_w2)
    y_prompt = trunk(x_prompt, mem_prompt, layer_stack, final_norm)
    y_sample = trunk(x_sample, mem_sample, layer_stack, final_norm)
    return (y_prompt, y_sample)
```

```python
import functools

import jax
import jax.numpy as jnp
from jax import lax
from jax.experimental import pallas as pl
from jax.experimental.pallas import tpu as pltpu

F32 = jnp.float32
BF16 = jnp.bfloat16

D_MODEL = 2048
N_MEM = 256
D_SSD = 2048
SSD_HEADS = 32
SSD_HEADDIM = 64
SSD_GROUPS = 4
SSD_STATE = 128
SSD_CHUNK = 128
CONV_WIDTH = 5
GLA_HEADS = 4
GLA_HEAD_K = 256
GLA_HEAD_V = 512
D_GLA_K = 1024
D_GLA_V = 2048
GLA_RANK = 16
GLA_NORMALIZER = 16.0
GLA_CHUNK = 64
XATTN_HEADS = 4
XATTN_HEAD_DIM = 512
D_FF = 5632
EPS = 1e-6

COL_Z = 0
COL_XS = 2048
COL_V = 4096
COL_GOUT = 6144
COL_BC = 8192
COL_Q = 9216
COL_K = 10240
COL_SMALL = 11264
N_PROJ = 11776
SMALL_W = 512

SUBLANE = 8
HALO = SUBLANE
TM_FFN = 512
TF_FFN = 512
TM_PROJ = 1024
TN_PROJ = 512
TM_OUT = 512
TN_OUT = 512
TM_XATTN = 512
GLA_BLOCK = 128
VMEM_LIMIT = 56 * 1024 * 1024


def _cparams(sem):
    return pltpu.CompilerParams(dimension_semantics=sem, vmem_limit_bytes=VMEM_LIMIT)


def _rms_normalize(x, w):
    ms = jnp.mean(x * x, axis=-1, keepdims=True)
    return x * lax.rsqrt(ms + EPS) * w


def _softplus(x):
    return jnp.maximum(x, 0.0) + jnp.log1p(jnp.exp(-jnp.abs(x)))


def _split_hi_lo(x):
    hi = x.astype(BF16)
    lo = (x - hi.astype(F32)).astype(BF16)
    return hi, lo


def _dot(a, b):
    return jnp.dot(a, b, preferred_element_type=F32)


def _dot_nt(a, b):
    return lax.dot_general(a, b, (((1,), (1,)), ((), ())), preferred_element_type=F32)


def _dot_tn(a, b):
    return lax.dot_general(a, b, (((0,), (0,)), ((), ())), preferred_element_type=F32)


def _ffn_kernel(x_ref, nw_ref, w1_ref, w3_ref, w2_ref, *rest, final):
    if final:
        fnw_ref, o_ref, h_ref, acc_ref = rest
    else:
        o_ref, h_ref, acc_ref = rest
    j = pl.program_id(1)

    @pl.when(j == 0)
    def _():
        h_ref[...] = _rms_normalize(x_ref[...], nw_ref[...]).astype(BF16)
        acc_ref[...] = jnp.zeros_like(acc_ref)

    h = h_ref[...]
    g = _dot(h, w1_ref[...])
    u = _dot(h, w3_ref[...])
    a = (g * jax.nn.sigmoid(g) * u).astype(BF16)
    acc_ref[...] += _dot(a, w2_ref[...])

    @pl.when(j == pl.num_programs(1) - 1)
    def _():
        y = x_ref[...] + 0.5 * acc_ref[...]
        if final:
            y = _rms_normalize(y, fnw_ref[...])
        o_ref[...] = y


def _ffn(x, nw, w1, w3, w2, final_nw=None):
    m, d = x.shape
    dff = w1.shape[1]
    tm, tf = TM_FFN, TF_FFN
    final = final_nw is not None
    in_specs = [
        pl.BlockSpec((tm, d), lambda i, j: (i, 0)),
        pl.BlockSpec((1, d), lambda i, j: (0, 0)),
        pl.BlockSpec((d, tf), lambda i, j: (0, j)),
        pl.BlockSpec((d, tf), lambda i, j: (0, j)),
        pl.BlockSpec((tf, d), lambda i, j: (j, 0)),
    ]
    args = [x, nw, w1, w3, w2]
    if final:
        in_specs.append(pl.BlockSpec((1, d), lambda i, j: (0, 0)))
        args.append(final_nw)
    return pl.pallas_call(
        functools.partial(_ffn_kernel, final=final),
        grid=(m // tm, dff // tf),
        in_specs=in_specs,
        out_specs=pl.BlockSpec((tm, d), lambda i, j: (i, 0)),
        out_shape=jax.ShapeDtypeStruct((m, d), F32),
        scratch_shapes=[pltpu.VMEM((tm, d), BF16), pltpu.VMEM((tm, d), F32)],
        compiler_params=_cparams(("parallel", "arbitrary")),
        name="ffn_final" if final else "ffn",
    )(*args)


def _norm_matmul_kernel(x_ref, nw_ref, w_ref, o_ref, h_ref):
    @pl.when(pl.program_id(1) == 0)
    def _():
        h_ref[...] = _rms_normalize(x_ref[...], nw_ref[...]).astype(BF16)

    o_ref[...] = _dot(h_ref[...], w_ref[...]).astype(o_ref.dtype)


def _norm_matmul(x, nw, w, out_dtype, tm, tn, name):
    m, d = x.shape
    n = w.shape[1]
    return pl.pallas_call(
        _norm_matmul_kernel,
        grid=(m // tm, n // tn),
        in_specs=[
            pl.BlockSpec((tm, d), lambda i, j: (i, 0)),
            pl.BlockSpec((1, d), lambda i, j: (0, 0)),
            pl.BlockSpec((d, tn), lambda i, j: (0, j)),
        ],
        out_specs=pl.BlockSpec((tm, tn), lambda i, j: (i, j)),
        out_shape=jax.ShapeDtypeStruct((m, n), out_dtype),
        scratch_shapes=[pltpu.VMEM((tm, d), BF16)],
        compiler_params=_cparams(("parallel", "arbitrary")),
        name=name,
    )(x, nw, w)


def _ssd_kernel(*refs, rev, cps, final):
    (xs_m, xs_p, xs_n, bc_m, bc_p, bc_n, small_ref) = refs[:7]
    k = 7
    if final:
        z_ref, yprev_ref = refs[k:k + 2]
        k += 2
    (cwx_ref, cbx_ref, cwbc_ref, cbbc_ref, dtb_ref, alog_ref, e_ref) = refs[k:k + 7]
    k += 7
    if final:
        dskip_ref, nw_ref = refs[k:k + 2]
        k += 2
    o_ref, s_ref = refs[k:k + 2]

    q = SSD_CHUNK
    step = pl.program_id(0)
    c = (pl.num_programs(0) - 1 - step) if rev else step
    pos = c % cps
    first = pos == ((cps - 1) if rev else 0)

    @pl.when(first)
    def _():
        s_ref[...] = jnp.zeros_like(s_ref)

    has_prev = pos != 0
    has_next = pos != cps - 1

    def conv_silu(m_ref, p_ref, n_ref, w_ref, b_ref):
        ext = jnp.concatenate(
            [jnp.where(has_prev, p_ref[...], 0.0), m_ref[...], jnp.where(has_next, n_ref[...], 0.0)],
            axis=0)
        rows = q + 2 * HALO
        acc = None
        for t in range(CONV_WIDTH):
            d = t - CONV_WIDTH // 2
            sh = ext if d == 0 else pltpu.roll(ext, shift=(-d) % rows, axis=0)
            term = sh[HALO:HALO + q] * w_ref[t:t + 1, :]
            acc = term if acc is None else acc + term
        acc = acc + b_ref[...]
        return acc * jax.nn.sigmoid(acc)

    xs = conv_silu(xs_m, xs_p, xs_n, cwx_ref, cbx_ref)
    bc = conv_silu(bc_m, bc_p, bc_n, cwbc_ref, cbbc_ref)
    ng = SSD_GROUPS * SSD_STATE
    bm = bc[:, :ng].astype(BF16)
    cm = bc[:, ng:].astype(BF16)

    off = SSD_HEADS if rev else 0
    sm_t = small_ref[:, 0:128].T
    dt_t = _softplus(sm_t[off:off + SSD_HEADS, :] + dtb_ref[...])
    a_t = dt_t * (-jnp.exp(alog_ref[...]))
    r_i = lax.broadcasted_iota(jnp.int32, (q, q), 0)
    c_i = lax.broadcasted_iota(jnp.int32, (q, q), 1)
    tri = (r_i >= c_i) if rev else (r_i <= c_i)
    tri_bf = jnp.where(tri, 1.0, 0.0).astype(BF16)
    a_hi, a_lo = _split_hi_lo(a_t)
    cum_t = _dot(a_hi, tri_bf) + _dot(a_lo, tri_bf)
    tot_t = cum_t[:, 0:1] if rev else cum_t[:, q - 1:q]
    wend_t = dt_t * jnp.exp(tot_t - cum_t)
    dec_t = jnp.exp(cum_t)
    vt = jnp.concatenate([dt_t, wend_t, dec_t, cum_t], axis=0)
    vv = vt.T
    vv_bf = vv.astype(BF16)
    dt_x = _dot(vv_bf, e_ref[0])
    wend_x = _dot(vv_bf, e_ref[1])
    dec_x = _dot(vv_bf, e_ref[2])
    xdt = (xs * dt_x).astype(BF16)
    xend = (xs * wend_x).astype(BF16)
    dec_tot = dec_x[0:1, :] if rev else dec_x[q - 1:q, :]

    causal = (c_i >= r_i) if rev else (c_i <= r_i)
    lane = lax.broadcasted_iota(jnp.int32, (q, 128), 1)
    lo_half = lane < SSD_HEADDIM
    hpg = SSD_HEADS // SSD_GROUPS
    gw = hpg * SSD_HEADDIM
    ys = []
    for g in range(SSD_GROUPS):
        cm_g = cm[:, g * SSD_STATE:(g + 1) * SSD_STATE]
        bm_g = bm[:, g * SSD_STATE:(g + 1) * SSD_STATE]
        cb = _dot_nt(cm_g, bm_g)
        y_off = _dot(cm_g, s_ref[g].astype(BF16)) * dec_x[:, g * gw:(g + 1) * gw]
        y_pairs = []
        for p in range(hpg // 2):
            ws = []
            for hh in range(2):
                h = g * hpg + 2 * p + hh
                col = 3 * SSD_HEADS + h
                seg = vv[:, col:col + 1] - cum_t[h:h + 1, :]
                ws.append((cb * jnp.exp(jnp.where(causal, seg, -jnp.inf))).astype(BF16))
            lhs = jnp.concatenate(ws, axis=1)
            l0 = g * gw + p * 128
            xp = xdt[:, l0:l0 + 128]
            zero = jnp.zeros_like(xp)
            rhs = jnp.concatenate([jnp.where(lo_half, xp, zero), jnp.where(lo_half, zero, xp)], axis=0)
            y_pairs.append(_dot(lhs, rhs))
        ys.append(jnp.concatenate(y_pairs, axis=1) + y_off)
        upd = _dot_tn(bm_g, xend[:, g * gw:(g + 1) * gw])
        s_ref[g] = s_ref[g] * dec_tot[:, g * gw:(g + 1) * gw] + upd
    y = jnp.concatenate(ys, axis=1)

    if not final:
        o_ref[...] = y
    else:
        y = y + yprev_ref[...] + dskip_ref[...] * xs
        zz = z_ref[...]
        y = y * (zz * jax.nn.sigmoid(zz))
        outs = []
        for g in range(SSD_GROUPS):
            yg = y[:, g * gw:(g + 1) * gw]
            outs.append(_rms_normalize(yg, nw_ref[:, g * gw:(g + 1) * gw]))
        o_ref[...] = jnp.concatenate(outs, axis=1).astype(o_ref.dtype)


def _ssd_sweep(proj, yprev, prm, *, rev, cps):
    m = proj.shape[0]
    q = SSD_CHUNK
    nc = m // q
    final = yprev is not None
    rb = q // HALO
    last_hb = m // HALO - 1

    def cidx(s):
        return (nc - 1 - s) if rev else s

    def main(colblk):
        return lambda s: (cidx(s), colblk)

    def prev(colblk):
        return lambda s: (jnp.maximum(cidx(s) * rb - 1, 0), colblk)

    def nxt(colblk):
        return lambda s: (jnp.minimum((cidx(s) + 1) * rb, last_hb), colblk)

    def const(shape):
        nd = len(shape)
        return pl.BlockSpec(shape, lambda s: (0,) * nd)

    xs_blk, bc_blk = COL_XS // 2048, COL_BC // 1024
    in_specs = [
        pl.BlockSpec((q, 2048), main(xs_blk)),
        pl.BlockSpec((HALO, 2048), prev(xs_blk)),
        pl.BlockSpec((HALO, 2048), nxt(xs_blk)),
        pl.BlockSpec((q, 1024), main(bc_blk)),
        pl.BlockSpec((HALO, 1024), prev(bc_blk)),
        pl.BlockSpec((HALO, 1024), nxt(bc_blk)),
        pl.BlockSpec((q, SMALL_W), main(COL_SMALL // SMALL_W)),
    ]
    args = [proj] * 7
    if final:
        in_specs += [pl.BlockSpec((q, 2048), main(COL_Z // 2048)), pl.BlockSpec((q, 2048), main(0))]
        args += [proj, yprev]
    dtb = prm["dtb_b"] if rev else prm["dtb_f"]
    alog = prm["alog_b"] if rev else prm["alog_f"]
    small_prm = [prm["cw_x"], prm["cb_x"], prm["cw_bc"], prm["cb_bc"], dtb, alog, prm["expand"]]
    if final:
        small_prm += [prm["dskip_x"], prm["ssd_nw"]]
    in_specs += [const(a.shape) for a in small_prm]
    args += small_prm
    out_dtype = BF16 if final else F32
    return pl.pallas_call(
        functools.partial(_ssd_kernel, rev=rev, cps=cps, final=final),
        grid=(nc,),
        in_specs=in_specs,
        out_specs=pl.BlockSpec((q, D_SSD), main(0)),
        out_shape=jax.ShapeDtypeStruct((m, D_SSD), out_dtype),
        scratch_shapes=[pltpu.VMEM((SSD_GROUPS, SSD_STATE, (SSD_HEADS // SSD_GROUPS) * SSD_HEADDIM), F32)],
        compiler_params=_cparams(("arbitrary",)),
        name="ssd_bwd" if rev else "ssd_fwd",
    )(*args)


def _gla_kernel(*refs, rev, bps, final):
    q_ref, k_ref, v_ref, small_ref = refs[:4]
    i = 4
    if final:
        gout_ref, oprev_ref = refs[i:i + 2]
        i += 2
    wg_ref, bg_ref = refs[i:i + 2]
    i += 2
    if final:
        nw_ref = refs[i]
        i += 1
    o_ref, st_ref = refs[i:i + 2]

    qc = GLA_CHUNK
    step = pl.program_id(0)
    blk = (pl.num_programs(0) - 1 - step) if rev else step
    pos = blk % bps
    first = pos == ((bps - 1) if rev else 0)

    @pl.when(first)
    def _():
        st_ref[...] = jnp.zeros_like(st_ref)

    r_i = lax.broadcasted_iota(jnp.int32, (qc, qc), 0)
    c_i = lax.broadcasted_iota(jnp.int32, (qc, qc), 1)
    tri = (c_i >= r_i) if rev else (c_i <= r_i)
    tri_bf = jnp.where(tri, 1.0, 0.0).astype(BF16)
    causal = tri
    mid = qc // 2 if rev else qc // 2 - 1
    last = 0 if rev else qc - 1
    scale = GLA_HEAD_K ** -0.5
    nsub = GLA_BLOCK // qc
    order = range(nsub - 1, -1, -1) if rev else range(nsub)
    for sc in order:
        r0 = sc * qc
        low = small_ref[r0:r0 + qc, 0:128].astype(BF16)
        pre = _dot(low, wg_ref[...]) + bg_ref[...]
        gk = -_softplus(-pre) / GLA_NORMALIZER
        g_hi, g_lo = _split_hi_lo(gk)
        gg = _dot(tri_bf, g_hi) + _dot(tri_bf, g_lo)
        gmid = gg[mid:mid + 1, :]
        glast = gg[last:last + 1, :]
        qv = q_ref[r0:r0 + qc, :] * scale
        kv = k_ref[r0:r0 + qc, :]
        qe = (qv * jnp.exp(gg - gmid)).astype(BF16)
        ke = (kv * jnp.exp(gmid - gg)).astype(BF16)
        qin = (qv * jnp.exp(gg)).astype(BF16)
        kend = (kv * jnp.exp(glast - gg)).astype(BF16)
        dec = jnp.exp(glast)
        vb = v_ref[r0:r0 + qc, :].astype(BF16)
        outs = []
        for h in range(GLA_HEADS):
            ks = slice(h * GLA_HEAD_K, (h + 1) * GLA_HEAD_K)
            vs = slice(h * GLA_HEAD_V, (h + 1) * GLA_HEAD_V)
            a = _dot_nt(qe[:, ks], ke[:, ks])
            a = jnp.where(causal, a, 0.0).astype(BF16)
            st = st_ref[h]
            o_h = _dot(a, vb[:, vs]) + _dot_nt(qin[:, ks], st.astype(BF16))
            st_ref[h] = st * dec[:, ks] + _dot_tn(vb[:, vs], kend[:, ks])
            if final:
                o_h = o_h + oprev_ref[r0:r0 + qc, vs]
                o_h = _rms_normalize(o_h, nw_ref[...])
                gz = gout_ref[r0:r0 + qc, vs]
                o_h = o_h * (gz * jax.nn.sigmoid(gz))
            outs.append(o_h)
        o_ref[r0:r0 + qc, :] = jnp.concatenate(outs, axis=1).astype(o_ref.dtype)


def _gla_sweep(proj, oprev, prm, *, rev, bps):
    m = proj.shape[0]
    tb = GLA_BLOCK
    nb = m // tb
    final = oprev is not None

    def bidx(s):
        return (nb - 1 - s) if rev else s

    def main(colblk):
        return lambda s: (bidx(s), colblk)

    def const(shape):
        nd = len(shape)
        return pl.BlockSpec(shape, lambda s: (0,) * nd)

    in_specs = [
        pl.BlockSpec((tb, D_GLA_K), main(COL_Q // D_GLA_K)),
        pl.BlockSpec((tb, D_GLA_K), main(COL_K // D_GLA_K)),
        pl.BlockSpec((tb, D_GLA_V), main(COL_V // D_GLA_V)),
        pl.BlockSpec((tb, SMALL_W), main(COL_SMALL // SMALL_W)),
    ]
    args = [proj] * 4
    if final:
        in_specs += [pl.BlockSpec((tb, D_GLA_V), main(COL_GOUT // D_GLA_V)),
                     pl.BlockSpec((tb, D_GLA_V), main(0))]
        args += [proj, oprev]
    small_prm = [prm["gate_w_b"] if rev else prm["gate_w_f"], prm["gate_b_b"] if rev else prm["gate_b_f"]]
    if final:
        small_prm.append(prm["gla_nw"])
    in_specs += [const(a.shape) for a in small_prm]
    args += small_prm
    out_dtype = BF16 if final else F32
    return pl.pallas_call(
        functools.partial(_gla_kernel, rev=rev, bps=bps, final=final),
        grid=(nb,),
        in_specs=in_specs,
        out_specs=pl.BlockSpec((tb, D_GLA_V), main(0)),
        out_shape=jax.ShapeDtypeStruct((m, D_GLA_V), out_dtype),
        scratch_shapes=[pltpu.VMEM((GLA_HEADS, GLA_HEAD_V, GLA_HEAD_K), F32)],
        compiler_params=_cparams(("arbitrary",)),
        name="gla_bwd" if rev else "gla_fwd",
    )(*args)


def _out_proj_kernel(x_ref, ya_ref, yb_ref, wa_ref, wb_ref, o_ref):
    o_ref[...] = x_ref[...] + _dot(ya_ref[...], wa_ref[...]) + _dot(yb_ref[...], wb_ref[...])


def _out_proj(x, ya, yb, wa, wb):
    m, d = x.shape
    ka = ya.shape[1]
    kb = yb.shape[1]
    tm, tn = TM_OUT, TN_OUT
    return pl.pallas_call(
        _out_proj_kernel,
        grid=(m // tm, d // tn),
        in_specs=[
            pl.BlockSpec((tm, tn), lambda i, j: (i, j)),
            pl.BlockSpec((tm, ka), lambda i, j: (i, 0)),
            pl.BlockSpec((tm, kb), lambda i, j: (i, 0)),
            pl.BlockSpec((ka, tn), lambda i, j: (0, j)),
            pl.BlockSpec((kb, tn), lambda i, j: (0, j)),
        ],
        out_specs=pl.BlockSpec((tm, tn), lambda i, j: (i, j)),
        out_shape=jax.ShapeDtypeStruct((m, d), F32),
        compiler_params=_cparams(("parallel", "arbitrary")),
        name="out_proj",
    )(x, ya, yb, wa, wb)


def _xattn_kernel(x_ref, nw_ref, wq_ref, kv_ref, wo_ref, o_ref):
    x = x_ref[...]
    h = _rms_normalize(x, nw_ref[...]).astype(BF16)
    qq = _dot(h, wq_ref[...]).astype(BF16)
    scale = XATTN_HEAD_DIM ** -0.5
    heads = []
    for hd in range(XATTN_HEADS):
        ds = slice(hd * XATTN_HEAD_DIM, (hd + 1) * XATTN_HEAD_DIM)
        kh = kv_ref[:, ds]
        vh = kv_ref[:, D_MODEL + hd * XATTN_HEAD_DIM:D_MODEL + (hd + 1) * XATTN_HEAD_DIM]
        s = _dot_nt(qq[:, ds], kh) * scale
        s = s - jnp.max(s, axis=-1, keepdims=True)
        e = jnp.exp(s)
        p = e / jnp.sum(e, axis=-1, keepdims=True)
        heads.append(_dot(p.astype(BF16), vh).astype(BF16))
    o = jnp.concatenate(heads, axis=1)
    o_ref[...] = x + _dot(o, wo_ref[...])


def _xattn(x, nw, wq, kv, wo, tiles_per_seq):
    m, d = x.shape
    tm = TM_XATTN
    single = pl.Buffered(1)
    return pl.pallas_call(
        _xattn_kernel,
        grid=(m // tm,),
        in_specs=[
            pl.BlockSpec((tm, d), lambda i: (i, 0)),
            pl.BlockSpec((1, d), lambda i: (0, 0)),
            pl.BlockSpec((d, d), lambda i: (0, 0), pipeline_mode=single),
            pl.BlockSpec((N_MEM, 2 * d), lambda i: (i // tiles_per_seq, 0)),
            pl.BlockSpec((d, d), lambda i: (0, 0), pipeline_mode=single),
        ],
        out_specs=pl.BlockSpec((tm, d), lambda i: (i, 0)),
        out_shape=jax.ShapeDtypeStruct((m, d), F32),
        compiler_params=_cparams(("arbitrary",)),
        name="xattn",
    )(x, nw, wq, kv, wo)


def _prepare(p):
    w_in = p["w_in"]
    d = w_in.shape[0]
    pad = jnp.zeros((d, SMALL_W - 2 * SSD_HEADS - 2 * GLA_RANK), w_in.dtype)
    w_perm = jnp.concatenate([
        w_in[:, 0:2048],
        w_in[:, 2048:4096],
        w_in[:, 7232:9280],
        w_in[:, 9312:11360],
        w_in[:, 4096:5120],
        w_in[:, 5184:6208],
        w_in[:, 6208:7232],
        w_in[:, 5120:5184],
        w_in[:, 9280:9312],
        pad], axis=1).astype(BF16)

    def gate_w(w, row0):
        full = jnp.zeros((128, D_GLA_K), F32)
        return full.at[row0:row0 + GLA_RANK].set(w).astype(BF16)

    head_of_lane = jnp.arange(D_SSD) // SSD_HEADDIM
    rows = jnp.arange(128)[:, None]
    expand = jnp.stack([(rows == (head_of_lane[None, :] + SSD_HEADS * t)) for t in range(3)], axis=0)
    conv_w, conv_b = p["conv_w"], p["conv_b"]
    return dict(
        w_perm=w_perm,
        cw_x=conv_w[:, :D_SSD], cb_x=conv_b[None, :D_SSD],
        cw_bc=conv_w[:, D_SSD:], cb_bc=conv_b[None, D_SSD:],
        dtb_f=p["dt_bias_fwd"][:, None], dtb_b=p["dt_bias_bwd"][:, None],
        alog_f=p["a_log_fwd"][:, None], alog_b=p["a_log_bwd"][:, None],
        expand=expand.astype(BF16),
        dskip_x=jnp.repeat(p["d_skip"], SSD_HEADDIM)[None, :],
        ssd_nw=p["ssd_norm"][None, :],
        gate_w_f=gate_w(p["gla_gate_w_fwd"], 2 * SSD_HEADS),
        gate_w_b=gate_w(p["gla_gate_w_bwd"], 2 * SSD_HEADS + GLA_RANK),
        gate_b_f=p["gla_gate_b_fwd"][None, :], gate_b_b=p["gla_gate_b_bwd"][None, :],
        gla_nw=p["gla_norm"][None, :],
        w_out_a=p["w_out"][:D_SSD].astype(BF16), w_out_b=p["w_out"][D_SSD:].astype(BF16),
        ffn1=(p["ffn1_norm"][None, :], p["ffn1_w1"].astype(BF16), p["ffn1_w3"].astype(BF16), p["ffn1_w2"].astype(BF16)),
        ffn2=(p["ffn2_norm"][None, :], p["ffn2_w1"].astype(BF16), p["ffn2_w3"].astype(BF16), p["ffn2_w2"].astype(BF16)),
        mix_nw=p["mix_norm"][None, :],
        xattn_nw=p["xattn_norm"][None, :], mem_nw=p["mem_norm"][None, :],
        w_cq=p["w_cq"].astype(BF16), w_ckv=p["w_ckv"].astype(BF16), w_co=p["w_co"].astype(BF16),
        final_nw=p["final_norm"][None, :],
    )


def _trunk(x3, mem3, prm):
    b, l, d = x3.shape
    x = x3.reshape(b * l, d)
    mem = mem3.reshape(b * N_MEM, d)
    x = _ffn(x, *prm["ffn1"])
    proj = _norm_matmul(x, prm["mix_nw"], prm["w_perm"], F32, TM_PROJ, TN_PROJ, "in_proj")
    y_f = _ssd_sweep(proj, None, prm, rev=False, cps=l // SSD_CHUNK)
    y = _ssd_sweep(proj, y_f, prm, rev=True, cps=l // SSD_CHUNK)
    o_f = _gla_sweep(proj, None, prm, rev=False, bps=l // GLA_BLOCK)
    o = _gla_sweep(proj, o_f, prm, rev=True, bps=l // GLA_BLOCK)
    x = _out_proj(x, y, o, prm["w_out_a"], prm["w_out_b"])
    kv = _norm_matmul(mem, prm["mem_nw"], prm["w_ckv"], BF16, N_MEM, 1024, "kv_proj")
    x = _xattn(x, prm["xattn_nw"], prm["w_cq"], kv, prm["w_co"], l // TM_XATTN)
    x = _ffn(x, *prm["ffn2"], final_nw=prm["final_nw"])
    return x.reshape(b, l, d)


def kernel(x_prompt, x_sample, mem_prompt, mem_sample, ffn1_norm, ffn1_w1, ffn1_w3, ffn1_w2, mix_norm, w_in, conv_w, conv_b, dt_bias_fwd, dt_bias_bwd, a_log_fwd, a_log_bwd, d_skip, ssd_norm, gla_gate_w_fwd, gla_gate_b_fwd, gla_gate_w_bwd, gla_gate_b_bwd, gla_norm, w_out, xattn_norm, mem_norm, w_cq, w_ckv, w_co, ffn2_norm, ffn2_w1, ffn2_w3, ffn2_w2, final_norm):
    p = dict(
        ffn1_norm=ffn1_norm[0], ffn1_w1=ffn1_w1[0], ffn1_w3=ffn1_w3[0], ffn1_w2=ffn1_w2[0],
        mix_norm=mix_norm[0], w_in=w_in[0], conv_w=conv_w[0], conv_b=conv_b[0],
        dt_bias_fwd=dt_bias_fwd[0], dt_bias_bwd=dt_bias_bwd[0], a_log_fwd=a_log_fwd[0], a_log_bwd=a_log_bwd[0],
        d_skip=d_skip[0], ssd_norm=ssd_norm[0],
        gla_gate_w_fwd=gla_gate_w_fwd[0], gla_gate_b_fwd=gla_gate_b_fwd[0],
        gla_gate_w_bwd=gla_gate_w_bwd[0], gla_gate_b_bwd=gla_gate_b_bwd[0],
        gla_norm=gla_norm[0], w_out=w_out[0], xattn_norm=xattn_norm[0], mem_norm=mem_norm[0],
        w_cq=w_cq[0], w_ckv=w_ckv[0], w_co=w_co[0],
        ffn2_norm=ffn2_norm[0], ffn2_w1=ffn2_w1[0], ffn2_w3=ffn2_w3[0], ffn2_w2=ffn2_w2[0],
        final_norm=final_norm,
    )
    prm = _prepare(p)
    return (_trunk(x_prompt, mem_prompt, prm), _trunk(x_sample, mem_sample, prm))
```

```python
import functools

import jax
import jax.numpy as jnp
from jax import lax
from jax.experimental import pallas as pl
from jax.experimental.pallas import tpu as pltpu

F32 = jnp.float32
BF16 = jnp.bfloat16

D_MODEL = 2048
N_MEM = 256
D_SSD = 2048
SSD_HEADS = 32
SSD_HEADDIM = 64
SSD_GROUPS = 4
SSD_STATE = 128
SSD_CHUNK = 128
CONV_WIDTH = 5
GLA_HEADS = 4
GLA_HEAD_K = 256
GLA_HEAD_V = 512
D_GLA_K = 1024
D_GLA_V = 2048
GLA_RANK = 16
GLA_NORMALIZER = 16.0
GLA_CHUNK = 64
XATTN_HEADS = 4
XATTN_HEAD_DIM = 512
D_FF = 5632
EPS = 1e-6

COL_Z = 0
COL_XS = 2048
COL_V = 4096
COL_GOUT = 6144
COL_BC = 8192
COL_Q = 9216
COL_K = 10240
COL_SMALL = 11264
N_PROJ = 11776
SMALL_W = 512

SUBLANE = 8
LANES = 128
HALO = SUBLANE
TM_FFN = 512
TF_FFN = 512
TM_PROJ = 1024
TN_PROJ = 512
TM_OUT = 512
TN_OUT = 512
TM_XATTN = 512
GLA_BLOCK = SSD_CHUNK
LOG2E = 1.4426950408889634
VMEM_LIMIT = 56 * 1024 * 1024


def _cparams(sem):
    return pltpu.CompilerParams(dimension_semantics=sem, vmem_limit_bytes=VMEM_LIMIT)


def _rms_normalize(x, w):
    ms = jnp.mean(x * x, axis=-1, keepdims=True)
    return x * lax.rsqrt(ms + EPS) * w


def _softplus(x):
    return jnp.maximum(x, 0.0) + jnp.log1p(jnp.exp(-jnp.abs(x)))


def _split_hi_lo(x):
    hi = x.astype(BF16)
    lo = (x - hi.astype(F32)).astype(BF16)
    return hi, lo


def _dot(a, b):
    return jnp.dot(a, b, preferred_element_type=F32)


def _dot_nt(a, b):
    return lax.dot_general(a, b, (((1,), (1,)), ((), ())), preferred_element_type=F32)


def _dot_tn(a, b):
    return lax.dot_general(a, b, (((0,), (0,)), ((), ())), preferred_element_type=F32)


def _ffn_kernel(x_ref, nw_ref, w1_ref, w3_ref, w2_ref, *rest, final):
    if final:
        fnw_ref, o_ref, h_ref, acc_ref = rest
    else:
        o_ref, h_ref, acc_ref = rest
    j = pl.program_id(1)

    @pl.when(j == 0)
    def _():
        h_ref[...] = _rms_normalize(x_ref[...], nw_ref[...]).astype(BF16)
        acc_ref[...] = jnp.zeros_like(acc_ref)

    h = h_ref[...]
    g = _dot(h, w1_ref[...])
    u = _dot(h, w3_ref[...])
    a = (g * jax.nn.sigmoid(g) * u).astype(BF16)
    acc_ref[...] += _dot(a, w2_ref[...])

    @pl.when(j == pl.num_programs(1) - 1)
    def _():
        y = x_ref[...] + 0.5 * acc_ref[...]
        if final:
            y = _rms_normalize(y, fnw_ref[...])
        o_ref[...] = y


def _ffn(x, nw, w1, w3, w2, final_nw=None):
    m, d = x.shape
    dff = w1.shape[1]
    tm, tf = TM_FFN, TF_FFN
    final = final_nw is not None
    in_specs = [
        pl.BlockSpec((tm, d), lambda i, j: (i, 0)),
        pl.BlockSpec((1, d), lambda i, j: (0, 0)),
        pl.BlockSpec((d, tf), lambda i, j: (0, j)),
        pl.BlockSpec((d, tf), lambda i, j: (0, j)),
        pl.BlockSpec((tf, d), lambda i, j: (j, 0)),
    ]
    args = [x, nw, w1, w3, w2]
    if final:
        in_specs.append(pl.BlockSpec((1, d), lambda i, j: (0, 0)))
        args.append(final_nw)
    return pl.pallas_call(
        functools.partial(_ffn_kernel, final=final),
        grid=(m // tm, dff // tf),
        in_specs=in_specs,
        out_specs=pl.BlockSpec((tm, d), lambda i, j: (i, 0)),
        out_shape=jax.ShapeDtypeStruct((m, d), F32),
        scratch_shapes=[pltpu.VMEM((tm, d), BF16), pltpu.VMEM((tm, d), F32)],
        compiler_params=_cparams(("parallel", "arbitrary")),
        name="ffn_final" if final else "ffn",
    )(*args)


def _norm_matmul_kernel(x_ref, nw_ref, w_ref, o_ref, h_ref):
    @pl.when(pl.program_id(1) == 0)
    def _():
        h_ref[...] = _rms_normalize(x_ref[...], nw_ref[...]).astype(BF16)

    o_ref[...] = _dot(h_ref[...], w_ref[...]).astype(o_ref.dtype)


def _norm_matmul(x, nw, w, out_dtype, tm, tn, name):
    m, d = x.shape
    n = w.shape[1]
    return pl.pallas_call(
        _norm_matmul_kernel,
        grid=(m // tm, n // tn),
        in_specs=[
            pl.BlockSpec((tm, d), lambda i, j: (i, 0)),
            pl.BlockSpec((1, d), lambda i, j: (0, 0)),
            pl.BlockSpec((d, tn), lambda i, j: (0, j)),
        ],
        out_specs=pl.BlockSpec((tm, tn), lambda i, j: (i, j)),
        out_shape=jax.ShapeDtypeStruct((m, n), out_dtype),
        scratch_shapes=[pltpu.VMEM((tm, d), BF16)],
        compiler_params=_cparams(("parallel", "arbitrary")),
        name=name,
    )(x, nw, w)


def _conv_silu(ext_ref, m_ref, p_ref, n_ref, w_ref, b_ref, has_prev, has_next):
    q = SSD_CHUNK
    nslab = m_ref.shape[1] // LANES
    for s in range(nslab):
        ls = slice(s * LANES, (s + 1) * LANES)
        ext_ref[s, 0:HALO, :] = jnp.where(has_prev, p_ref[:, ls], 0.0)
        ext_ref[s, HALO:HALO + q, :] = m_ref[:, ls]
        ext_ref[s, HALO + q:HALO + q + HALO, :] = jnp.where(has_next, n_ref[:, ls], 0.0)
    outs = []
    for s in range(nslab):
        ls = slice(s * LANES, (s + 1) * LANES)
        acc = b_ref[:, ls]
        for t in range(CONV_WIDTH):
            r0 = HALO + t - CONV_WIDTH // 2
            acc = acc + ext_ref[s, pl.ds(r0, q, stride=1), :] * w_ref[t:t + 1, ls]
        outs.append(acc * jax.nn.sigmoid(acc))
    return jnp.concatenate(outs, axis=1)


def _ssd_core(xs, bm, cm, small_ref, dtb_ref, alog_ref, e_ref, s_ref, rev):
    q = SSD_CHUNK
    off = SSD_HEADS if rev else 0
    sm_t = small_ref[:, 0:128].T
    dt_t = _softplus(sm_t[off:off + SSD_HEADS, :] + dtb_ref[...])
    a_t = dt_t * (-jnp.exp(alog_ref[...]))
    r_i = lax.broadcasted_iota(jnp.int32, (q, q), 0)
    c_i = lax.broadcasted_iota(jnp.int32, (q, q), 1)
    tri = (r_i >= c_i) if rev else (r_i <= c_i)
    tri_bf = jnp.where(tri, 1.0, 0.0).astype(BF16)
    a_hi, a_lo = _split_hi_lo(a_t)
    cum_t = _dot(a_hi, tri_bf) + _dot(a_lo, tri_bf)
    tot_t = cum_t[:, 0:1] if rev else cum_t[:, q - 1:q]
    wend_t = dt_t * jnp.exp(tot_t - cum_t)
    dec_t = jnp.exp(cum_t)
    cum2_t = cum_t * LOG2E
    vt = jnp.concatenate([dt_t, wend_t, dec_t, cum2_t], axis=0)
    vv = vt.T
    vv_bf = vv.astype(BF16)
    dt_x = _dot(vv_bf, e_ref[0])
    wend_x = _dot(vv_bf, e_ref[1])
    dec_x = _dot(vv_bf, e_ref[2])
    xdt = (xs * dt_x).astype(BF16)
    xend = (xs * wend_x).astype(BF16)
    dec_tot = dec_x[0:1, :] if rev else dec_x[q - 1:q, :]

    causal = (c_i >= r_i) if rev else (c_i <= r_i)
    lane = lax.broadcasted_iota(jnp.int32, (q, 128), 1)
    lo_half = lane < SSD_HEADDIM
    hpg = SSD_HEADS // SSD_GROUPS
    gw = hpg * SSD_HEADDIM
    ys = []
    for g in range(SSD_GROUPS):
        cm_g = cm[:, g * SSD_STATE:(g + 1) * SSD_STATE]
        bm_g = bm[:, g * SSD_STATE:(g + 1) * SSD_STATE]
        cb = _dot_nt(cm_g, bm_g)
        y_off = _dot(cm_g, s_ref[g].astype(BF16)) * dec_x[:, g * gw:(g + 1) * gw]
        y_pairs = []
        for p in range(hpg // 2):
            ws = []
            for hh in range(2):
                h = g * hpg + 2 * p + hh
                col = 3 * SSD_HEADS + h
                seg = vv[:, col:col + 1] - cum2_t[h:h + 1, :]
                ws.append((cb * jnp.exp2(jnp.where(causal, seg, -jnp.inf))).astype(BF16))
            lhs = jnp.concatenate(ws, axis=1)
            l0 = g * gw + p * 128
            xp = xdt[:, l0:l0 + 128]
            zero = jnp.zeros_like(xp)
            rhs = jnp.concatenate([jnp.where(lo_half, xp, zero), jnp.where(lo_half, zero, xp)], axis=0)
            y_pairs.append(_dot(lhs, rhs))
        ys.append(jnp.concatenate(y_pairs, axis=1) + y_off)
        upd = _dot_tn(bm_g, xend[:, g * gw:(g + 1) * gw])
        s_ref[g] = s_ref[g] * dec_tot[:, g * gw:(g + 1) * gw] + upd
    return jnp.concatenate(ys, axis=1)


def _gla_core(q_ref, k_ref, v_ref, small_ref, wg_ref, bg_ref, s_ref, rev):
    tb, qc = GLA_BLOCK, GLA_CHUNK
    r_i = lax.broadcasted_iota(jnp.int32, (tb, tb), 0)
    c_i = lax.broadcasted_iota(jnp.int32, (tb, tb), 1)
    r_blk = r_i // qc
    c_blk = c_i // qc
    order = (c_i >= r_i) if rev else (c_i <= r_i)
    diag_f = jnp.where(r_blk == c_blk, jnp.where(order, 1.0, 0.0), 0.0)
    diag = diag_f > 0.5
    offd = (c_blk - r_blk == 1) if rev else (r_blk - c_blk == 1)
    tri_bf = diag_f.astype(BF16)

    low = small_ref[:, 0:128].astype(BF16)
    pre = _dot(low, wg_ref[...]) + bg_ref[...]
    gk2 = _softplus(-pre) * (-LOG2E / GLA_NORMALIZER)
    g_hi, g_lo = _split_hi_lo(gk2)
    gg = _dot(tri_bf, g_hi) + _dot(tri_bf, g_lo)

    def rows2(a0, a1):
        n = a0.shape[1]
        return jnp.concatenate([jnp.broadcast_to(a0, (qc, n)), jnp.broadcast_to(a1, (qc, n))], axis=0)

    mid = qc // 2 if rev else qc // 2 - 1
    last = 0 if rev else qc - 1
    gmid = rows2(gg[mid:mid + 1, :], gg[qc + mid:qc + mid + 1, :])
    gl0 = gg[last:last + 1, :]
    gl1 = gg[qc + last:qc + last + 1, :]
    glast = rows2(gl0, gl1)
    dec0 = jnp.exp2(gl0)
    dec1 = jnp.exp2(gl1)
    dec_tot = jnp.exp2(gl0 + gl1)

    qv = q_ref[...] * (GLA_HEAD_K ** -0.5)
    kv = k_ref[...]
    qe = (qv * jnp.exp2(gg - gmid)).astype(BF16)
    ke = (kv * jnp.exp2(gmid - gg)).astype(BF16)
    qin = qv * jnp.exp2(gg)
    kend = kv * jnp.exp2(glast - gg)
    if rev:
        qin_x = jnp.concatenate([qin[:qc] * dec1, qin[qc:]], axis=0)
        kend_x = jnp.concatenate([kend[:qc], kend[qc:] * dec0], axis=0)
    else:
        qin_x = jnp.concatenate([qin[:qc], qin[qc:] * dec0], axis=0)
        kend_x = jnp.concatenate([kend[:qc] * dec1, kend[qc:]], axis=0)
    qin = qin.astype(BF16)
    kend = kend.astype(BF16)
    qin_x = qin_x.astype(BF16)
    kend_x = kend_x.astype(BF16)
    vb = v_ref[...].astype(BF16)
    decb = jnp.broadcast_to(dec_tot, (128, D_GLA_K))
    outs = []
    for h in range(GLA_HEADS):
        ks = slice(h * GLA_HEAD_K, (h + 1) * GLA_HEAD_K)
        vs = slice(h * GLA_HEAD_V, (h + 1) * GLA_HEAD_V)
        a_diag = _dot_nt(qe[:, ks], ke[:, ks])
        a_off = _dot_nt(qin[:, ks], kend[:, ks])
        a = jnp.where(diag, a_diag, jnp.where(offd, a_off, 0.0)).astype(BF16)
        s = s_ref[h]
        outs.append(_dot(a, vb[:, vs]) + _dot(qin_x[:, ks], s.astype(BF16)))
        upd = _dot_tn(kend_x[:, ks], vb[:, vs])
        dcol = jnp.concatenate(
            [decb[:, h * GLA_HEAD_K + t * 128:h * GLA_HEAD_K + (t + 1) * 128].T for t in range(GLA_HEAD_K // 128)],
            axis=0)
        s_ref[h] = s * jnp.concatenate([dcol] * (GLA_HEAD_V // 128), axis=1) + upd
    return outs


def _chunk_position(rev, cps):
    step = pl.program_id(0)
    c = (pl.num_programs(0) - 1 - step) if rev else step
    pos = c % cps
    first = pos == ((cps - 1) if rev else 0)
    return pos, first


def _mix_fwd_kernel(xs_m, xs_p, xs_n, bc_m, bc_p, bc_n, small_ref, q_ref, k_ref, v_ref,
                    cwx_ref, cbx_ref, cwbc_ref, cbbc_ref, dtb_ref, alog_ref, e_ref, wg_ref, bg_ref,
                    yf_ref, xsc_ref, bcc_ref, of_ref,
                    s_ssd, s_gla, ext_x, ext_bc, *, cps):
    pos, first = _chunk_position(False, cps)

    @pl.when(first)
    def _():
        s_ssd[...] = jnp.zeros_like(s_ssd)
        s_gla[...] = jnp.zeros_like(s_gla)

    has_prev = pos != 0
    has_next = pos != cps - 1
    xs = _conv_silu(ext_x, xs_m, xs_p, xs_n, cwx_ref, cbx_ref, has_prev, has_next)
    bc = _conv_silu(ext_bc, bc_m, bc_p, bc_n, cwbc_ref, cbbc_ref, has_prev, has_next).astype(BF16)
    xsc_ref[...] = xs
    bcc_ref[...] = bc
    ng = SSD_GROUPS * SSD_STATE
    yf_ref[...] = _ssd_core(xs, bc[:, :ng], bc[:, ng:], small_ref, dtb_ref, alog_ref, e_ref, s_ssd, False)
    outs = _gla_core(q_ref, k_ref, v_ref, small_ref, wg_ref, bg_ref, s_gla, False)
    of_ref[...] = jnp.concatenate(outs, axis=1)


def _mix_bwd_kernel(xsc_ref, bcc_ref, small_ref, z_ref, yf_ref, q_ref, k_ref, v_ref, gout_ref, of_ref,
                    dtb_ref, alog_ref, e_ref, dskip_ref, snw_ref, wg_ref, bg_ref, gnw_ref,
                    y_ref, o_ref,
                    s_ssd, s_gla, *, cps):
    _, first = _chunk_position(True, cps)

    @pl.when(first)
    def _():
        s_ssd[...] = jnp.zeros_like(s_ssd)
        s_gla[...] = jnp.zeros_like(s_gla)

    xs = xsc_ref[...]
    bc = bcc_ref[...]
    ng = SSD_GROUPS * SSD_STATE
    y = _ssd_core(xs, bc[:, :ng], bc[:, ng:], small_ref, dtb_ref, alog_ref, e_ref, s_ssd, True)
    y = y + yf_ref[...] + dskip_ref[...] * xs
    zz = z_ref[...]
    y = y * (zz * jax.nn.sigmoid(zz))
    gw = D_SSD // SSD_GROUPS
    y_ref[...] = jnp.concatenate(
        [_rms_normalize(y[:, g * gw:(g + 1) * gw], snw_ref[:, g * gw:(g + 1) * gw]) for g in range(SSD_GROUPS)],
        axis=1).astype(y_ref.dtype)

    outs = _gla_core(q_ref, k_ref, v_ref, small_ref, wg_ref, bg_ref, s_gla, True)
    fin = []
    for h in range(GLA_HEADS):
        vs = slice(h * GLA_HEAD_V, (h + 1) * GLA_HEAD_V)
        o_h = _rms_normalize(outs[h] + of_ref[:, vs], gnw_ref[...])
        gz = gout_ref[:, vs]
        fin.append(o_h * (gz * jax.nn.sigmoid(gz)))
    o_ref[...] = jnp.concatenate(fin, axis=1).astype(o_ref.dtype)


def _const_spec(a):
    nd = a.ndim
    return pl.BlockSpec(a.shape, lambda s: (0,) * nd)


_MIX_SCRATCH = [
    pltpu.VMEM((SSD_GROUPS, SSD_STATE, D_SSD // SSD_GROUPS), F32),
    pltpu.VMEM((GLA_HEADS, GLA_HEAD_K, GLA_HEAD_V), F32),
]


def _mix_fwd(proj, prm, cps):
    m = proj.shape[0]
    q = SSD_CHUNK
    nc = m // q
    rb = q // HALO
    last_hb = m // HALO - 1

    def main(colblk):
        return lambda s: (s, colblk)

    def prev(colblk):
        return lambda s: (jnp.maximum(s * rb - 1, 0), colblk)

    def nxt(colblk):
        return lambda s: (jnp.minimum((s + 1) * rb, last_hb), colblk)

    xs_blk, bc_blk = COL_XS // 2048, COL_BC // 1024
    in_specs = [
        pl.BlockSpec((q, 2048), main(xs_blk)),
        pl.BlockSpec((HALO, 2048), prev(xs_blk)),
        pl.BlockSpec((HALO, 2048), nxt(xs_blk)),
        pl.BlockSpec((q, 1024), main(bc_blk)),
        pl.BlockSpec((HALO, 1024), prev(bc_blk)),
        pl.BlockSpec((HALO, 1024), nxt(bc_blk)),
        pl.BlockSpec((q, SMALL_W), main(COL_SMALL // SMALL_W)),
        pl.BlockSpec((q, D_GLA_K), main(COL_Q // D_GLA_K)),
        pl.BlockSpec((q, D_GLA_K), main(COL_K // D_GLA_K)),
        pl.BlockSpec((q, D_GLA_V), main(COL_V // D_GLA_V)),
    ]
    params = [prm["cw_x"], prm["cb_x"], prm["cw_bc"], prm["cb_bc"], prm["dtb_f"], prm["alog_f"], prm["expand"],
              prm["gate_w_f"], prm["gate_b_f"]]
    in_specs += [_const_spec(a) for a in params]
    return pl.pallas_call(
        functools.partial(_mix_fwd_kernel, cps=cps),
        grid=(nc,),
        in_specs=in_specs,
        out_specs=[
            pl.BlockSpec((q, D_SSD), main(0)),
            pl.BlockSpec((q, D_SSD), main(0)),
            pl.BlockSpec((q, 2 * SSD_GROUPS * SSD_STATE), main(0)),
            pl.BlockSpec((q, D_GLA_V), main(0)),
        ],
        out_shape=[
            jax.ShapeDtypeStruct((m, D_SSD), F32),
            jax.ShapeDtypeStruct((m, D_SSD), F32),
            jax.ShapeDtypeStruct((m, 2 * SSD_GROUPS * SSD_STATE), BF16),
            jax.ShapeDtypeStruct((m, D_GLA_V), F32),
        ],
        scratch_shapes=_MIX_SCRATCH + [
            pltpu.VMEM((D_SSD // LANES, q + 2 * HALO, LANES), F32),
            pltpu.VMEM((2 * SSD_GROUPS * SSD_STATE // LANES, q + 2 * HALO, LANES), F32),
        ],
        compiler_params=_cparams(("arbitrary",)),
        name="mix_fwd",
    )(*([proj] * 10), *params)


def _mix_bwd(proj, y_f, xs_c, bc_c, o_f, prm, cps):
    m = proj.shape[0]
    q = SSD_CHUNK
    nc = m // q

    def main(colblk):
        return lambda s: (nc - 1 - s, colblk)

    in_specs = [
        pl.BlockSpec((q, D_SSD), main(0)),
        pl.BlockSpec((q, 2 * SSD_GROUPS * SSD_STATE), main(0)),
        pl.BlockSpec((q, SMALL_W), main(COL_SMALL // SMALL_W)),
        pl.BlockSpec((q, 2048), main(COL_Z // 2048)),
        pl.BlockSpec((q, D_SSD), main(0)),
        pl.BlockSpec((q, D_GLA_K), main(COL_Q // D_GLA_K)),
        pl.BlockSpec((q, D_GLA_K), main(COL_K // D_GLA_K)),
        pl.BlockSpec((q, D_GLA_V), main(COL_V // D_GLA_V)),
        pl.BlockSpec((q, D_GLA_V), main(COL_GOUT // D_GLA_V)),
        pl.BlockSpec((q, D_GLA_V), main(0)),
    ]
    params = [prm["dtb_b"], prm["alog_b"], prm["expand"], prm["dskip_x"], prm["ssd_nw"],
              prm["gate_w_b"], prm["gate_b_b"], prm["gla_nw"]]
    in_specs += [_const_spec(a) for a in params]
    return pl.pallas_call(
        functools.partial(_mix_bwd_kernel, cps=cps),
        grid=(nc,),
        in_specs=in_specs,
        out_specs=[pl.BlockSpec((q, D_SSD), main(0)), pl.BlockSpec((q, D_GLA_V), main(0))],
        out_shape=[jax.ShapeDtypeStruct((m, D_SSD), BF16), jax.ShapeDtypeStruct((m, D_GLA_V), BF16)],
        scratch_shapes=_MIX_SCRATCH,
        compiler_params=_cparams(("arbitrary",)),
        name="mix_bwd",
    )(xs_c, bc_c, proj, proj, y_f, proj, proj, proj, proj, o_f, *params)


def _out_proj_kernel(x_ref, ya_ref, yb_ref, wa_ref, wb_ref, o_ref):
    o_ref[...] = x_ref[...] + _dot(ya_ref[...], wa_ref[...]) + _dot(yb_ref[...], wb_ref[...])


def _out_proj(x, ya, yb, wa, wb):
    m, d = x.shape
    ka = ya.shape[1]
    kb = yb.shape[1]
    tm, tn = TM_OUT, TN_OUT
    return pl.pallas_call(
        _out_proj_kernel,
        grid=(m // tm, d // tn),
        in_specs=[
            pl.BlockSpec((tm, tn), lambda i, j: (i, j)),
            pl.BlockSpec((tm, ka), lambda i, j: (i, 0)),
            pl.BlockSpec((tm, kb), lambda i, j: (i, 0)),
            pl.BlockSpec((ka, tn), lambda i, j: (0, j)),
            pl.BlockSpec((kb, tn), lambda i, j: (0, j)),
        ],
        out_specs=pl.BlockSpec((tm, tn), lambda i, j: (i, j)),
        out_shape=jax.ShapeDtypeStruct((m, d), F32),
        compiler_params=_cparams(("parallel", "arbitrary")),
        name="out_proj",
    )(x, ya, yb, wa, wb)


def _xattn_kernel(x_ref, nw_ref, wq_ref, kv_ref, wo_ref, o_ref):
    x = x_ref[...]
    h = _rms_normalize(x, nw_ref[...]).astype(BF16)
    qq = _dot(h, wq_ref[...]).astype(BF16)
    scale = XATTN_HEAD_DIM ** -0.5
    heads = []
    for hd in range(XATTN_HEADS):
        ds = slice(hd * XATTN_HEAD_DIM, (hd + 1) * XATTN_HEAD_DIM)
        kh = kv_ref[:, ds]
        vh = kv_ref[:, D_MODEL + hd * XATTN_HEAD_DIM:D_MODEL + (hd + 1) * XATTN_HEAD_DIM]
        s = _dot_nt(qq[:, ds], kh) * scale
        s = s - jnp.max(s, axis=-1, keepdims=True)
        e = jnp.exp(s)
        p = e / jnp.sum(e, axis=-1, keepdims=True)
        heads.append(_dot(p.astype(BF16), vh).astype(BF16))
    o = jnp.concatenate(heads, axis=1)
    o_ref[...] = x + _dot(o, wo_ref[...])


def _xattn(x, nw, wq, kv, wo, tiles_per_seq):
    m, d = x.shape
    tm = TM_XATTN
    single = pl.Buffered(1)
    return pl.pallas_call(
        _xattn_kernel,
        grid=(m // tm,),
        in_specs=[
            pl.BlockSpec((tm, d), lambda i: (i, 0)),
            pl.BlockSpec((1, d), lambda i: (0, 0)),
            pl.BlockSpec((d, d), lambda i: (0, 0), pipeline_mode=single),
            pl.BlockSpec((N_MEM, 2 * d), lambda i: (i // tiles_per_seq, 0)),
            pl.BlockSpec((d, d), lambda i: (0, 0), pipeline_mode=single),
        ],
        out_specs=pl.BlockSpec((tm, d), lambda i: (i, 0)),
        out_shape=jax.ShapeDtypeStruct((m, d), F32),
        compiler_params=_cparams(("arbitrary",)),
        name="xattn",
    )(x, nw, wq, kv, wo)


def _prepare(p):
    w_in = p["w_in"]
    d = w_in.shape[0]
    pad = jnp.zeros((d, SMALL_W - 2 * SSD_HEADS - 2 * GLA_RANK), w_in.dtype)
    w_perm = jnp.concatenate([
        w_in[:, 0:2048],
        w_in[:, 2048:4096],
        w_in[:, 7232:9280],
        w_in[:, 9312:11360],
        w_in[:, 4096:5120],
        w_in[:, 5184:6208],
        w_in[:, 6208:7232],
        w_in[:, 5120:5184],
        w_in[:, 9280:9312],
        pad], axis=1).astype(BF16)

    def gate_w(w, row0):
        full = jnp.zeros((128, D_GLA_K), F32)
        return full.at[row0:row0 + GLA_RANK].set(w).astype(BF16)

    head_of_lane = jnp.arange(D_SSD) // SSD_HEADDIM
    rows = jnp.arange(128)[:, None]
    expand = jnp.stack([(rows == (head_of_lane[None, :] + SSD_HEADS * t)) for t in range(3)], axis=0)
    conv_w, conv_b = p["conv_w"], p["conv_b"]
    return dict(
        w_perm=w_perm,
        cw_x=conv_w[:, :D_SSD], cb_x=conv_b[None, :D_SSD],
        cw_bc=conv_w[:, D_SSD:], cb_bc=conv_b[None, D_SSD:],
        dtb_f=p["dt_bias_fwd"][:, None], dtb_b=p["dt_bias_bwd"][:, None],
        alog_f=p["a_log_fwd"][:, None], alog_b=p["a_log_bwd"][:, None],
        expand=expand.astype(BF16),
        dskip_x=jnp.repeat(p["d_skip"], SSD_HEADDIM)[None, :],
        ssd_nw=p["ssd_norm"][None, :],
        gate_w_f=gate_w(p["gla_gate_w_fwd"], 2 * SSD_HEADS),
        gate_w_b=gate_w(p["gla_gate_w_bwd"], 2 * SSD_HEADS + GLA_RANK),
        gate_b_f=p["gla_gate_b_fwd"][None, :], gate_b_b=p["gla_gate_b_bwd"][None, :],
        gla_nw=p["gla_norm"][None, :],
        w_out_a=p["w_out"][:D_SSD].astype(BF16), w_out_b=p["w_out"][D_SSD:].astype(BF16),
        ffn1=(p["ffn1_norm"][None, :], p["ffn1_w1"].astype(BF16), p["ffn1_w3"].astype(BF16), p["ffn1_w2"].astype(BF16)),
        ffn2=(p["ffn2_norm"][None, :], p["ffn2_w1"].astype(BF16), p["ffn2_w3"].astype(BF16), p["ffn2_w2"].astype(BF16)),
        mix_nw=p["mix_norm"][None, :],
        xattn_nw=p["xattn_norm"][None, :], mem_nw=p["mem_norm"][None, :],
        w_cq=p["w_cq"].astype(BF16), w_ckv=p["w_ckv"].astype(BF16), w_co=p["w_co"].astype(BF16),
        final_nw=p["final_norm"][None, :],
    )


def _trunk(x3, mem3, prm):
    b, l, d = x3.shape
    x = x3.reshape(b * l, d)
    mem = mem3.reshape(b * N_MEM, d)
    x = _ffn(x, *prm["ffn1"])
    proj = _norm_matmul(x, prm["mix_nw"], prm["w_perm"], F32, TM_PROJ, TN_PROJ, "in_proj")
    y_f, xs_c, bc_c, o_f = _mix_fwd(proj, prm, l // SSD_CHUNK)
    y, o = _mix_bwd(proj, y_f, xs_c, bc_c, o_f, prm, l // SSD_CHUNK)
    x = _out_proj(x, y, o, prm["w_out_a"], prm["w_out_b"])
    kv = _norm_matmul(mem, prm["mem_nw"], prm["w_ckv"], BF16, N_MEM, 1024, "kv_proj")
    x = _xattn(x, prm["xattn_nw"], prm["w_cq"], kv, prm["w_co"], l // TM_XATTN)
    x = _ffn(x, *prm["ffn2"], final_nw=prm["final_nw"])
    return x.reshape(b, l, d)


def kernel(x_prompt, x_sample, mem_prompt, mem_sample, ffn1_norm, ffn1_w1, ffn1_w3, ffn1_w2, mix_norm, w_in, conv_w, conv_b, dt_bias_fwd, dt_bias_bwd, a_log_fwd, a_log_bwd, d_skip, ssd_norm, gla_gate_w_fwd, gla_gate_b_fwd, gla_gate_w_bwd, gla_gate_b_bwd, gla_norm, w_out, xattn_norm, mem_norm, w_cq, w_ckv, w_co, ffn2_norm, ffn2_w1, ffn2_w3, ffn2_w2, final_norm):
    p = dict(
        ffn1_norm=ffn1_norm[0], ffn1_w1=ffn1_w1[0], ffn1_w3=ffn1_w3[0], ffn1_w2=ffn1_w2[0],
        mix_norm=mix_norm[0], w_in=w_in[0], conv_w=conv_w[0], conv_b=conv_b[0],
        dt_bias_fwd=dt_bias_fwd[0], dt_bias_bwd=dt_bias_bwd[0], a_log_fwd=a_log_fwd[0], a_log_bwd=a_log_bwd[0],
        d_skip=d_skip[0], ssd_norm=ssd_norm[0],
        gla_gate_w_fwd=gla_gate_w_fwd[0], gla_gate_b_fwd=gla_gate_b_fwd[0],
        gla_gate_w_bwd=gla_gate_w_bwd[0], gla_gate_b_bwd=gla_gate_b_bwd[0],
        gla_norm=gla_norm[0], w_out=w_out[0], xattn_norm=xattn_norm[0], mem_norm=mem_norm[0],
        w_cq=w_cq[0], w_ckv=w_ckv[0], w_co=w_co[0],
        ffn2_norm=ffn2_norm[0], ffn2_w1=ffn2_w1[0], ffn2_w3=ffn2_w3[0], ffn2_w2=ffn2_w2[0],
        final_norm=final_norm,
    )
    prm = _prepare(p)
    return (_trunk(x_prompt, mem_prompt, prm), _trunk(x_sample, mem_sample, prm))
```

```python
import functools

import jax
import jax.numpy as jnp
from jax import lax
from jax.experimental import pallas as pl
from jax.experimental.pallas import tpu as pltpu

F32 = jnp.float32
BF16 = jnp.bfloat16

D_MODEL = 2048
N_MEM = 256
D_SSD = 2048
SSD_HEADS = 32
SSD_HEADDIM = 64
SSD_GROUPS = 4
SSD_STATE = 128
SSD_CHUNK = 128
CONV_WIDTH = 5
GLA_HEADS = 4
GLA_HEAD_K = 256
GLA_HEAD_V = 512
D_GLA_K = 1024
D_GLA_V = 2048
GLA_RANK = 16
GLA_NORMALIZER = 16.0
GLA_CHUNK = 64
XATTN_HEADS = 4
XATTN_HEAD_DIM = 512
D_FF = 5632
EPS = 1e-6

COL_Z = 0
COL_XS = 2048
COL_V = 4096
COL_GOUT = 6144
COL_BC = 8192
COL_Q = 9216
COL_K = 10240
COL_SMALL = 11264
N_PROJ = 11776
SMALL_W = 512

SUBLANE = 8
LANES = 128
HALO = SUBLANE
TM_FFN = 1024
TF_FFN = 512
FFN_ROW_CHUNK = 256
FFN_COL_CHUNK = 512
TM_PROJ = 1024
TN_PROJ = 512
TM_OUT = 1024
TN_OUT = 512
TM_XATTN = 512
GLA_BLOCK = SSD_CHUNK
LOG2E = 1.4426950408889634
VMEM_LIMIT = 56 * 1024 * 1024
VMEM_LIMIT_FFN = 62 * 1024 * 1024


def _cparams(sem, vmem_limit=VMEM_LIMIT):
    return pltpu.CompilerParams(dimension_semantics=sem, vmem_limit_bytes=vmem_limit)


def _rms_normalize(x, w):
    ms = jnp.mean(x * x, axis=-1, keepdims=True)
    return x * lax.rsqrt(ms + EPS) * w


def _softplus(x):
    return jnp.maximum(x, 0.0) + jnp.log(1.0 + jnp.exp(-jnp.abs(x)))


def _split_hi_lo(x):
    hi = x.astype(BF16)
    lo = (x - hi.astype(F32)).astype(BF16)
    return hi, lo


def _dot(a, b):
    return jnp.dot(a, b, preferred_element_type=F32)


def _dot_nt(a, b):
    return lax.dot_general(a, b, (((1,), (1,)), ((), ())), preferred_element_type=F32)


def _dot_tn(a, b):
    return lax.dot_general(a, b, (((0,), (0,)), ((), ())), preferred_element_type=F32)


def _ffn_kernel(x_ref, nw_ref, w1_ref, w3_ref, w2_ref, *rest, final):
    if final:
        fnw_ref, o_ref, h_ref = rest
    else:
        o_ref, h_ref = rest
    j = pl.program_id(1)

    tm, d = o_ref.shape

    def for_row_chunks(body):
        def step(r, carry):
            body(pl.ds(pl.multiple_of(r * FFN_ROW_CHUNK, FFN_ROW_CHUNK), FFN_ROW_CHUNK))
            return carry
        lax.fori_loop(0, tm // FFN_ROW_CHUNK, step, 0)

    @pl.when(j == 0)
    def _():
        def body(rs):
            h_ref[rs, :] = _rms_normalize(x_ref[rs, :], nw_ref[...]).astype(BF16)
            o_ref[rs, :] = jnp.zeros((FFN_ROW_CHUNK, d), F32)
        for_row_chunks(body)

    h = h_ref[...]
    g = _dot(h, w1_ref[...])
    u = _dot(h, w3_ref[...])
    a = (g * jax.nn.sigmoid(g) * u).astype(BF16)
    for c in range(0, d, FFN_COL_CHUNK):
        cs = slice(c, c + FFN_COL_CHUNK)
        o_ref[:, cs] += _dot(a, w2_ref[:, cs])

    @pl.when(j == pl.num_programs(1) - 1)
    def _():
        def body(rs):
            y = x_ref[rs, :] + 0.5 * o_ref[rs, :]
            if final:
                y = _rms_normalize(y, fnw_ref[...])
            o_ref[rs, :] = y
        for_row_chunks(body)


def _ffn(x, nw, w1, w3, w2, final_nw=None):
    m, d = x.shape
    dff = w1.shape[1]
    tm, tf = TM_FFN, TF_FFN
    final = final_nw is not None
    in_specs = [
        pl.BlockSpec((tm, d), lambda i, j: (i, 0)),
        pl.BlockSpec((1, d), lambda i, j: (0, 0)),
        pl.BlockSpec((d, tf), lambda i, j: (0, j)),
        pl.BlockSpec((d, tf), lambda i, j: (0, j)),
        pl.BlockSpec((tf, d), lambda i, j: (j, 0)),
    ]
    args = [x, nw, w1, w3, w2]
    if final:
        in_specs.append(pl.BlockSpec((1, d), lambda i, j: (0, 0)))
        args.append(final_nw)
    return pl.pallas_call(
        functools.partial(_ffn_kernel, final=final),
        grid=(m // tm, dff // tf),
        in_specs=in_specs,
        out_specs=pl.BlockSpec((tm, d), lambda i, j: (i, 0)),
        out_shape=jax.ShapeDtypeStruct((m, d), F32),
        scratch_shapes=[pltpu.VMEM((tm, d), BF16)],
        compiler_params=_cparams(("parallel", "arbitrary"), VMEM_LIMIT_FFN),
        name="ffn_final" if final else "ffn",
    )(*args)


def _norm_matmul_kernel(x_ref, nw_ref, w_ref, o_ref, h_ref):
    @pl.when(pl.program_id(1) == 0)
    def _():
        h_ref[...] = _rms_normalize(x_ref[...], nw_ref[...]).astype(BF16)

    o_ref[...] = _dot(h_ref[...], w_ref[...]).astype(o_ref.dtype)


def _norm_matmul(x, nw, w, out_dtype, tm, tn, name):
    m, d = x.shape
    n = w.shape[1]
    return pl.pallas_call(
        _norm_matmul_kernel,
        grid=(m // tm, n // tn),
        in_specs=[
            pl.BlockSpec((tm, d), lambda i, j: (i, 0)),
            pl.BlockSpec((1, d), lambda i, j: (0, 0)),
            pl.BlockSpec((d, tn), lambda i, j: (0, j)),
        ],
        out_specs=pl.BlockSpec((tm, tn), lambda i, j: (i, j)),
        out_shape=jax.ShapeDtypeStruct((m, n), out_dtype),
        scratch_shapes=[pltpu.VMEM((tm, d), BF16)],
        compiler_params=_cparams(("parallel", "arbitrary")),
        name=name,
    )(x, nw, w)


def _conv_silu(ext_ref, m_ref, p_ref, n_ref, w_ref, b_ref, has_prev, has_next):
    q = SSD_CHUNK
    nslab = m_ref.shape[1] // LANES
    for s in range(nslab):
        ls = slice(s * LANES, (s + 1) * LANES)
        ext_ref[s, 0:HALO, :] = jnp.where(has_prev, p_ref[:, ls], 0.0)
        ext_ref[s, HALO:HALO + q, :] = m_ref[:, ls]
        ext_ref[s, HALO + q:HALO + q + HALO, :] = jnp.where(has_next, n_ref[:, ls], 0.0)
    outs = []
    for s in range(nslab):
        ls = slice(s * LANES, (s + 1) * LANES)
        acc = b_ref[:, ls]
        for t in range(CONV_WIDTH):
            r0 = HALO + t - CONV_WIDTH // 2
            acc = acc + ext_ref[s, pl.ds(r0, q, stride=1), :] * w_ref[t:t + 1, ls]
        outs.append(acc * jax.nn.sigmoid(acc))
    return jnp.concatenate(outs, axis=1)


def _ssd_pre(small_ref, dtb_ref, alog_ref, rev):
    q = SSD_CHUNK
    off = SSD_HEADS if rev else 0
    sm_t = small_ref[:, 0:128].T
    dt_t = _softplus(sm_t[off:off + SSD_HEADS, :] + dtb_ref[...])
    a_t = dt_t * (-jnp.exp(alog_ref[...]))
    r_i = lax.broadcasted_iota(jnp.int32, (q, q), 0)
    c_i = lax.broadcasted_iota(jnp.int32, (q, q), 1)
    tri = (r_i >= c_i) if rev else (r_i <= c_i)
    tri_bf = jnp.where(tri, 1.0, 0.0).astype(BF16)
    a_hi, a_lo = _split_hi_lo(a_t)
    cum_t = _dot(a_hi, tri_bf) + _dot(a_lo, tri_bf)
    tot_t = cum_t[:, 0:1] if rev else cum_t[:, q - 1:q]
    wend_t = dt_t * jnp.exp(tot_t - cum_t)
    dec_t = jnp.exp(cum_t)
    cum2_t = cum_t * LOG2E
    vt = jnp.concatenate([dt_t, wend_t, dec_t, cum2_t], axis=0)
    vv = vt.T
    lane = lax.broadcasted_iota(jnp.int32, (q, LANES), 1)
    return dict(
        vv=vv, vv_bf=vv.astype(BF16), cum2_t=cum2_t,
        causal=(c_i >= r_i) if rev else (c_i <= r_i),
        lo_half=lane < SSD_HEADDIM)


def _ssd_group(g, ctx, xs, bm, cm, e_ref, s_ref, rev):
    q = SSD_CHUNK
    hpg = SSD_HEADS // SSD_GROUPS
    gw = hpg * SSD_HEADDIM
    gs = slice(g * gw, (g + 1) * gw)
    vv, vv_bf, cum2_t = ctx["vv"], ctx["vv_bf"], ctx["cum2_t"]
    xs_g = xs[:, gs]
    xdt = (xs_g * _dot(vv_bf, e_ref[0, :, gs])).astype(BF16)
    xend = (xs_g * _dot(vv_bf, e_ref[1, :, gs])).astype(BF16)
    dec_x = _dot(vv_bf, e_ref[2, :, gs])
    dec_tot = dec_x[0:1, :] if rev else dec_x[q - 1:q, :]
    cm_g = cm[:, g * SSD_STATE:(g + 1) * SSD_STATE]
    bm_g = bm[:, g * SSD_STATE:(g + 1) * SSD_STATE]
    cb = _dot_nt(cm_g, bm_g)
    y_off = _dot(cm_g, s_ref[g].astype(BF16)) * dec_x
    y_pairs = []
    for p in range(hpg // 2):
        ws = []
        for hh in range(2):
            h = g * hpg + 2 * p + hh
            col = 3 * SSD_HEADS + h
            seg = vv[:, col:col + 1] - cum2_t[h:h + 1, :]
            ws.append((cb * jnp.exp2(jnp.where(ctx["causal"], seg, -jnp.inf))).astype(BF16))
        lhs = jnp.concatenate(ws, axis=1)
        xp = xdt[:, p * LANES:(p + 1) * LANES]
        zero = jnp.zeros_like(xp)
        rhs = jnp.concatenate([jnp.where(ctx["lo_half"], xp, zero), jnp.where(ctx["lo_half"], zero, xp)], axis=0)
        y_pairs.append(_dot(lhs, rhs))
    s_ref[g] = s_ref[g] * dec_tot + _dot_tn(bm_g, xend)
    return jnp.concatenate(y_pairs, axis=1) + y_off


def _gla_pre(q_ref, k_ref, v_ref, small_ref, wg_ref, bg_ref, rev):
    tb, qc = GLA_BLOCK, GLA_CHUNK
    r_i = lax.broadcasted_iota(jnp.int32, (tb, tb), 0)
    c_i = lax.broadcasted_iota(jnp.int32, (tb, tb), 1)
    r_blk = r_i // qc
    c_blk = c_i // qc
    order = (c_i >= r_i) if rev else (c_i <= r_i)
    diag_f = jnp.where(r_blk == c_blk, jnp.where(order, 1.0, 0.0), 0.0)
    tri_bf = diag_f.astype(BF16)

    low = small_ref[:, 0:128].astype(BF16)
    pre = _dot(low, wg_ref[...]) + bg_ref[...]
    gk2 = _softplus(-pre) * (-LOG2E / GLA_NORMALIZER)
    g_hi, g_lo = _split_hi_lo(gk2)
    gg = _dot(tri_bf, g_hi) + _dot(tri_bf, g_lo)

    def rows2(a0, a1):
        n = a0.shape[1]
        return jnp.concatenate([jnp.broadcast_to(a0, (qc, n)), jnp.broadcast_to(a1, (qc, n))], axis=0)

    mid = qc // 2 if rev else qc // 2 - 1
    last = 0 if rev else qc - 1
    gmid = rows2(gg[mid:mid + 1, :], gg[qc + mid:qc + mid + 1, :])
    gl0 = gg[last:last + 1, :]
    gl1 = gg[qc + last:qc + last + 1, :]
    glast = rows2(gl0, gl1)
    dec0 = jnp.exp2(gl0)
    dec1 = jnp.exp2(gl1)
    dec_tot = jnp.exp2(gl0 + gl1)

    qv = q_ref[...] * (GLA_HEAD_K ** -0.5)
    kv = k_ref[...]
    qe = (qv * jnp.exp2(gg - gmid)).astype(BF16)
    ke = (kv * jnp.exp2(gmid - gg)).astype(BF16)
    qin = qv * jnp.exp2(gg)
    kend = kv * jnp.exp2(glast - gg)
    if rev:
        qin_x = jnp.concatenate([qin[:qc] * dec1, qin[qc:]], axis=0)
        kend_x = jnp.concatenate([kend[:qc], kend[qc:] * dec0], axis=0)
    else:
        qin_x = jnp.concatenate([qin[:qc], qin[qc:] * dec0], axis=0)
        kend_x = jnp.concatenate([kend[:qc] * dec1, kend[qc:]], axis=0)
    return dict(
        qe=qe, ke=ke, qin=qin.astype(BF16), kend=kend.astype(BF16),
        qin_x=qin_x.astype(BF16), kend_x=kend_x.astype(BF16),
        vb=v_ref[...].astype(BF16),
        decb=jnp.broadcast_to(dec_tot, (LANES, D_GLA_K)),
        diag=diag_f > 0.5,
        offd=(c_blk - r_blk == 1) if rev else (r_blk - c_blk == 1))


def _gla_head(h, ctx, s_ref):
    ks = slice(h * GLA_HEAD_K, (h + 1) * GLA_HEAD_K)
    vs = slice(h * GLA_HEAD_V, (h + 1) * GLA_HEAD_V)
    a_diag = _dot_nt(ctx["qe"][:, ks], ctx["ke"][:, ks])
    a_off = _dot_nt(ctx["qin"][:, ks], ctx["kend"][:, ks])
    a = jnp.where(ctx["diag"], a_diag, jnp.where(ctx["offd"], a_off, 0.0)).astype(BF16)
    s = s_ref[h]
    vb_h = ctx["vb"][:, vs]
    o_h = _dot(a, vb_h) + _dot(ctx["qin_x"][:, ks], s.astype(BF16))
    upd = _dot_tn(ctx["kend_x"][:, ks], vb_h)
    decb = ctx["decb"]
    dcol = jnp.concatenate(
        [decb[:, h * GLA_HEAD_K + t * LANES:h * GLA_HEAD_K + (t + 1) * LANES].T for t in range(GLA_HEAD_K // LANES)],
        axis=0)
    s_ref[h] = s * jnp.concatenate([dcol] * (GLA_HEAD_V // LANES), axis=1) + upd
    return o_h


def _chunk_position(rev, cps):
    step = pl.program_id(0)
    c = (pl.num_programs(0) - 1 - step) if rev else step
    pos = c % cps
    first = pos == ((cps - 1) if rev else 0)
    return pos, first


def _mix_fwd_kernel(xs_m, xs_p, xs_n, bc_m, bc_p, bc_n, small_ref, q_ref, k_ref, v_ref,
                    cwx_ref, cbx_ref, cwbc_ref, cbbc_ref, dtb_ref, alog_ref, e_ref, wg_ref, bg_ref,
                    yf_ref, xsc_ref, bcc_ref, of_ref,
                    s_ssd, s_gla, ext_x, ext_bc, *, cps):
    pos, first = _chunk_position(False, cps)

    @pl.when(first)
    def _():
        s_ssd[...] = jnp.zeros_like(s_ssd)
        s_gla[...] = jnp.zeros_like(s_gla)

    has_prev = pos != 0
    has_next = pos != cps - 1
    xs = _conv_silu(ext_x, xs_m, xs_p, xs_n, cwx_ref, cbx_ref, has_prev, has_next)
    bc = _conv_silu(ext_bc, bc_m, bc_p, bc_n, cwbc_ref, cbbc_ref, has_prev, has_next).astype(BF16)
    xsc_ref[...] = xs
    bcc_ref[...] = bc
    sctx = _ssd_pre(small_ref, dtb_ref, alog_ref, False)
    gctx = _gla_pre(q_ref, k_ref, v_ref, small_ref, wg_ref, bg_ref, False)
    ng = SSD_GROUPS * SSD_STATE
    gw = D_SSD // SSD_GROUPS
    for i in range(SSD_GROUPS):
        of_ref[:, i * GLA_HEAD_V:(i + 1) * GLA_HEAD_V] = _gla_head(i, gctx, s_gla)
        yf_ref[:, i * gw:(i + 1) * gw] = _ssd_group(i, sctx, xs, bc[:, :ng], bc[:, ng:], e_ref, s_ssd, False)


def _mix_bwd_kernel(xsc_ref, bcc_ref, small_ref, z_ref, yf_ref, q_ref, k_ref, v_ref, gout_ref, of_ref,
                    dtb_ref, alog_ref, e_ref, dskip_ref, snw_ref, wg_ref, bg_ref, gnw_ref,
                    y_ref, o_ref,
                    s_ssd, s_gla, *, cps):
    _, first = _chunk_position(True, cps)

    @pl.when(first)
    def _():
        s_ssd[...] = jnp.zeros_like(s_ssd)
        s_gla[...] = jnp.zeros_like(s_gla)

    sctx = _ssd_pre(small_ref, dtb_ref, alog_ref, True)
    gctx = _gla_pre(q_ref, k_ref, v_ref, small_ref, wg_ref, bg_ref, True)
    xs = xsc_ref[...]
    bc = bcc_ref[...]
    ng = SSD_GROUPS * SSD_STATE
    gw = D_SSD // SSD_GROUPS
    for i in range(SSD_GROUPS):
        vs = slice(i * GLA_HEAD_V, (i + 1) * GLA_HEAD_V)
        o_h = _rms_normalize(_gla_head(i, gctx, s_gla) + of_ref[:, vs], gnw_ref[...])
        gz = gout_ref[:, vs]
        o_ref[:, vs] = (o_h * (gz * jax.nn.sigmoid(gz))).astype(o_ref.dtype)

        gs = slice(i * gw, (i + 1) * gw)
        y = _ssd_group(i, sctx, xs, bc[:, :ng], bc[:, ng:], e_ref, s_ssd, True)
        y = y + yf_ref[:, gs] + dskip_ref[:, gs] * xs[:, gs]
        zz = z_ref[:, gs]
        y = y * (zz * jax.nn.sigmoid(zz))
        y_ref[:, gs] = _rms_normalize(y, snw_ref[:, gs]).astype(y_ref.dtype)


def _const_spec(a):
    nd = a.ndim
    return pl.BlockSpec(a.shape, lambda s: (0,) * nd)


_MIX_SCRATCH = [
    pltpu.VMEM((SSD_GROUPS, SSD_STATE, D_SSD // SSD_GROUPS), F32),
    pltpu.VMEM((GLA_HEADS, GLA_HEAD_K, GLA_HEAD_V), F32),
]


def _mix_fwd(proj, prm, cps):
    m = proj.shape[0]
    q = SSD_CHUNK
    nc = m // q
    rb = q // HALO
    last_hb = m // HALO - 1

    def main(colblk):
        return lambda s: (s, colblk)

    def prev(colblk):
        return lambda s: (jnp.maximum(s * rb - 1, 0), colblk)

    def nxt(colblk):
        return lambda s: (jnp.minimum((s + 1) * rb, last_hb), colblk)

    xs_blk, bc_blk = COL_XS // 2048, COL_BC // 1024
    in_specs = [
        pl.BlockSpec((q, 2048), main(xs_blk)),
        pl.BlockSpec((HALO, 2048), prev(xs_blk)),
        pl.BlockSpec((HALO, 2048), nxt(xs_blk)),
        pl.BlockSpec((q, 1024), main(bc_blk)),
        pl.BlockSpec((HALO, 1024), prev(bc_blk)),
        pl.BlockSpec((HALO, 1024), nxt(bc_blk)),
        pl.BlockSpec((q, SMALL_W), main(COL_SMALL // SMALL_W)),
        pl.BlockSpec((q, D_GLA_K), main(COL_Q // D_GLA_K)),
        pl.BlockSpec((q, D_GLA_K), main(COL_K // D_GLA_K)),
        pl.BlockSpec((q, D_GLA_V), main(COL_V // D_GLA_V)),
    ]
    params = [prm["cw_x"], prm["cb_x"], prm["cw_bc"], prm["cb_bc"], prm["dtb_f"], prm["alog_f"], prm["expand"],
              prm["gate_w_f"], prm["gate_b_f"]]
    in_specs += [_const_spec(a) for a in params]
    return pl.pallas_call(
        functools.partial(_mix_fwd_kernel, cps=cps),
        grid=(nc,),
        in_specs=in_specs,
        out_specs=[
            pl.BlockSpec((q, D_SSD), main(0)),
            pl.BlockSpec((q, D_SSD), main(0)),
            pl.BlockSpec((q, 2 * SSD_GROUPS * SSD_STATE), main(0)),
            pl.BlockSpec((q, D_GLA_V), main(0)),
        ],
        out_shape=[
            jax.ShapeDtypeStruct((m, D_SSD), F32),
            jax.ShapeDtypeStruct((m, D_SSD), F32),
            jax.ShapeDtypeStruct((m, 2 * SSD_GROUPS * SSD_STATE), BF16),
            jax.ShapeDtypeStruct((m, D_GLA_V), F32),
        ],
        scratch_shapes=_MIX_SCRATCH + [
            pltpu.VMEM((D_SSD // LANES, q + 2 * HALO, LANES), F32),
            pltpu.VMEM((2 * SSD_GROUPS * SSD_STATE // LANES, q + 2 * HALO, LANES), F32),
        ],
        compiler_params=_cparams(("arbitrary",)),
        name="mix_fwd",
    )(*([proj] * 10), *params)


def _mix_bwd(proj, y_f, xs_c, bc_c, o_f, prm, cps):
    m = proj.shape[0]
    q = SSD_CHUNK
    nc = m // q

    def main(colblk):
        return lambda s: (nc - 1 - s, colblk)

    in_specs = [
        pl.BlockSpec((q, D_SSD), main(0)),
        pl.BlockSpec((q, 2 * SSD_GROUPS * SSD_STATE), main(0)),
        pl.BlockSpec((q, SMALL_W), main(COL_SMALL // SMALL_W)),
        pl.BlockSpec((q, 2048), main(COL_Z // 2048)),
        pl.BlockSpec((q, D_SSD), main(0)),
        pl.BlockSpec((q, D_GLA_K), main(COL_Q // D_GLA_K)),
        pl.BlockSpec((q, D_GLA_K), main(COL_K // D_GLA_K)),
        pl.BlockSpec((q, D_GLA_V), main(COL_V // D_GLA_V)),
        pl.BlockSpec((q, D_GLA_V), main(COL_GOUT // D_GLA_V)),
        pl.BlockSpec((q, D_GLA_V), main(0)),
    ]
    params = [prm["dtb_b"], prm["alog_b"], prm["expand"], prm["dskip_x"], prm["ssd_nw"],
              prm["gate_w_b"], prm["gate_b_b"], prm["gla_nw"]]
    in_specs += [_const_spec(a) for a in params]
    return pl.pallas_call(
        functools.partial(_mix_bwd_kernel, cps=cps),
        grid=(nc,),
        in_specs=in_specs,
        out_specs=[pl.BlockSpec((q, D_SSD), main(0)), pl.BlockSpec((q, D_GLA_V), main(0))],
        out_shape=[jax.ShapeDtypeStruct((m, D_SSD), BF16), jax.ShapeDtypeStruct((m, D_GLA_V), BF16)],
        scratch_shapes=_MIX_SCRATCH,
        compiler_params=_cparams(("arbitrary",)),
        name="mix_bwd",
    )(xs_c, bc_c, proj, proj, y_f, proj, proj, proj, proj, o_f, *params)


def _out_proj_kernel(x_ref, ya_ref, yb_ref, wa_ref, wb_ref, o_ref):
    o_ref[...] = x_ref[...] + _dot(ya_ref[...], wa_ref[...]) + _dot(yb_ref[...], wb_ref[...])


def _out_proj(x, ya, yb, wa, wb):
    m, d = x.shape
    ka = ya.shape[1]
    kb = yb.shape[1]
    tm, tn = TM_OUT, TN_OUT
    return pl.pallas_call(
        _out_proj_kernel,
        grid=(m // tm, d // tn),
        in_specs=[
            pl.BlockSpec((tm, tn), lambda i, j: (i, j)),
            pl.BlockSpec((tm, ka), lambda i, j: (i, 0)),
            pl.BlockSpec((tm, kb), lambda i, j: (i, 0)),
            pl.BlockSpec((ka, tn), lambda i, j: (0, j)),
            pl.BlockSpec((kb, tn), lambda i, j: (0, j)),
        ],
        out_specs=pl.BlockSpec((tm, tn), lambda i, j: (i, j)),
        out_shape=jax.ShapeDtypeStruct((m, d), F32),
        compiler_params=_cparams(("parallel", "arbitrary")),
        name="out_proj",
    )(x, ya, yb, wa, wb)


def _xattn_kernel(x_ref, nw_ref, wq_ref, kv_ref, wo_ref, o_ref):
    x = x_ref[...]
    h = _rms_normalize(x, nw_ref[...]).astype(BF16)
    qq = _dot(h, wq_ref[...]).astype(BF16)
    scale = XATTN_HEAD_DIM ** -0.5
    heads = []
    for hd in range(XATTN_HEADS):
        ds = slice(hd * XATTN_HEAD_DIM, (hd + 1) * XATTN_HEAD_DIM)
        kh = kv_ref[:, ds]
        vh = kv_ref[:, D_MODEL + hd * XATTN_HEAD_DIM:D_MODEL + (hd + 1) * XATTN_HEAD_DIM]
        s = _dot_nt(qq[:, ds], kh) * scale
        s = s - jnp.max(s, axis=-1, keepdims=True)
        e = jnp.exp(s)
        p = e / jnp.sum(e, axis=-1, keepdims=True)
        heads.append(_dot(p.astype(BF16), vh).astype(BF16))
    o = jnp.concatenate(heads, axis=1)
    o_ref[...] = x + _dot(o, wo_ref[...])


def _xattn(x, nw, wq, kv, wo, tiles_per_seq):
    m, d = x.shape
    tm = TM_XATTN
    single = pl.Buffered(1)
    return pl.pallas_call(
        _xattn_kernel,
        grid=(m // tm,),
        in_specs=[
            pl.BlockSpec((tm, d), lambda i: (i, 0)),
            pl.BlockSpec((1, d), lambda i: (0, 0)),
            pl.BlockSpec((d, d), lambda i: (0, 0), pipeline_mode=single),
            pl.BlockSpec((N_MEM, 2 * d), lambda i: (i // tiles_per_seq, 0)),
            pl.BlockSpec((d, d), lambda i: (0, 0), pipeline_mode=single),
        ],
        out_specs=pl.BlockSpec((tm, d), lambda i: (i, 0)),
        out_shape=jax.ShapeDtypeStruct((m, d), F32),
        compiler_params=_cparams(("arbitrary",)),
        name="xattn",
    )(x, nw, wq, kv, wo)


def _prepare(p):
    w_in = p["w_in"].astype(BF16)
    d = w_in.shape[0]
    pad = jnp.zeros((d, SMALL_W - 2 * SSD_HEADS - 2 * GLA_RANK), BF16)
    w_perm = jnp.concatenate([
        w_in[:, 0:2048],
        w_in[:, 2048:4096],
        w_in[:, 7232:9280],
        w_in[:, 9312:11360],
        w_in[:, 4096:5120],
        w_in[:, 5184:6208],
        w_in[:, 6208:7232],
        w_in[:, 5120:5184],
        w_in[:, 9280:9312],
        pad], axis=1)

    def gate_w(w, row0):
        full = jnp.zeros((128, D_GLA_K), F32)
        return full.at[row0:row0 + GLA_RANK].set(w).astype(BF16)

    head_of_lane = jnp.arange(D_SSD) // SSD_HEADDIM
    rows = jnp.arange(128)[:, None]
    expand = jnp.stack([(rows == (head_of_lane[None, :] + SSD_HEADS * t)) for t in range(3)], axis=0)
    conv_w, conv_b = p["conv_w"], p["conv_b"]
    return dict(
        w_perm=w_perm,
        cw_x=conv_w[:, :D_SSD], cb_x=conv_b[None, :D_SSD],
        cw_bc=conv_w[:, D_SSD:], cb_bc=conv_b[None, D_SSD:],
        dtb_f=p["dt_bias_fwd"][:, None], dtb_b=p["dt_bias_bwd"][:, None],
        alog_f=p["a_log_fwd"][:, None], alog_b=p["a_log_bwd"][:, None],
        expand=expand.astype(BF16),
        dskip_x=jnp.repeat(p["d_skip"], SSD_HEADDIM)[None, :],
        ssd_nw=p["ssd_norm"][None, :],
        gate_w_f=gate_w(p["gla_gate_w_fwd"], 2 * SSD_HEADS),
        gate_w_b=gate_w(p["gla_gate_w_bwd"], 2 * SSD_HEADS + GLA_RANK),
        gate_b_f=p["gla_gate_b_fwd"][None, :], gate_b_b=p["gla_gate_b_bwd"][None, :],
        gla_nw=p["gla_norm"][None, :],
        w_out_a=p["w_out"][:D_SSD].astype(BF16), w_out_b=p["w_out"][D_SSD:].astype(BF16),
        ffn1=(p["ffn1_norm"][None, :], p["ffn1_w1"].astype(BF16), p["ffn1_w3"].astype(BF16), p["ffn1_w2"].astype(BF16)),
        ffn2=(p["ffn2_norm"][None, :], p["ffn2_w1"].astype(BF16), p["ffn2_w3"].astype(BF16), p["ffn2_w2"].astype(BF16)),
        mix_nw=p["mix_norm"][None, :],
        xattn_nw=p["xattn_norm"][None, :], mem_nw=p["mem_norm"][None, :],
        w_cq=p["w_cq"].astype(BF16), w_ckv=p["w_ckv"].astype(BF16), w_co=p["w_co"].astype(BF16),
        final_nw=p["final_norm"][None, :],
    )


def _trunk(x3, mem3, prm):
    b, l, d = x3.shape
    x = x3.reshape(b * l, d)
    mem = mem3.reshape(b * N_MEM, d)
    x = _ffn(x, *prm["ffn1"])
    proj = _norm_matmul(x, prm["mix_nw"], prm["w_perm"], F32, TM_PROJ, TN_PROJ, "in_proj")
    y_f, xs_c, bc_c, o_f = _mix_fwd(proj, prm, l // SSD_CHUNK)
    y, o = _mix_bwd(proj, y_f, xs_c, bc_c, o_f, prm, l // SSD_CHUNK)
    x = _out_proj(x, y, o, prm["w_out_a"], prm["w_out_b"])
    kv = _norm_matmul(mem, prm["mem_nw"], prm["w_ckv"], BF16, N_MEM, 1024, "kv_proj")
    x = _xattn(x, prm["xattn_nw"], prm["w_cq"], kv, prm["w_co"], l // TM_XATTN)
    x = _ffn(x, *prm["ffn2"], final_nw=prm["final_nw"])
    return x.reshape(b, l, d)


def kernel(x_prompt, x_sample, mem_prompt, mem_sample, ffn1_norm, ffn1_w1, ffn1_w3, ffn1_w2, mix_norm, w_in, conv_w, conv_b, dt_bias_fwd, dt_bias_bwd, a_log_fwd, a_log_bwd, d_skip, ssd_norm, gla_gate_w_fwd, gla_gate_b_fwd, gla_gate_w_bwd, gla_gate_b_bwd, gla_norm, w_out, xattn_norm, mem_norm, w_cq, w_ckv, w_co, ffn2_norm, ffn2_w1, ffn2_w3, ffn2_w2, final_norm):
    p = dict(
        ffn1_norm=ffn1_norm[0], ffn1_w1=ffn1_w1[0], ffn1_w3=ffn1_w3[0], ffn1_w2=ffn1_w2[0],
        mix_norm=mix_norm[0], w_in=w_in[0], conv_w=conv_w[0], conv_b=conv_b[0],
        dt_bias_fwd=dt_bias_fwd[0], dt_bias_bwd=dt_bias_bwd[0], a_log_fwd=a_log_fwd[0], a_log_bwd=a_log_bwd[0],
        d_skip=d_skip[0], ssd_norm=ssd_norm[0],
        gla_gate_w_fwd=gla_gate_w_fwd[0], gla_gate_b_fwd=gla_gate_b_fwd[0],
        gla_gate_w_bwd=gla_gate_w_bwd[0], gla_gate_b_bwd=gla_gate_b_bwd[0],
        gla_norm=gla_norm[0], w_out=w_out[0], xattn_norm=xattn_norm[0], mem_norm=mem_norm[0],
        w_cq=w_cq[0], w_ckv=w_ckv[0], w_co=w_co[0],
        ffn2_norm=ffn2_norm[0], ffn2_w1=ffn2_w1[0], ffn2_w3=ffn2_w3[0], ffn2_w2=ffn2_w2[0],
        final_norm=final_norm,
    )
    prm = _prepare(p)
    return (_trunk(x_prompt, mem_prompt, prm), _trunk(x_sample, mem_sample, prm))
```

```python
import functools

import jax
import jax.numpy as jnp
from jax import lax
from jax.experimental import pallas as pl
from jax.experimental.pallas import tpu as pltpu

F32 = jnp.float32
BF16 = jnp.bfloat16

D_MODEL = 2048
N_MEM = 256
D_SSD = 2048
SSD_HEADS = 32
SSD_HEADDIM = 64
SSD_GROUPS = 4
SSD_STATE = 128
SSD_CHUNK = 128
CONV_WIDTH = 5
GLA_HEADS = 4
GLA_HEAD_K = 256
GLA_HEAD_V = 512
D_GLA_K = 1024
D_GLA_V = 2048
GLA_RANK = 16
GLA_NORMALIZER = 16.0
GLA_CHUNK = 64
XATTN_HEADS = 4
XATTN_HEAD_DIM = 512
D_FF = 5632
EPS = 1e-6

COL_Z = 0
COL_XS = 2048
COL_V = 4096
COL_GOUT = 6144
COL_BC = 8192
COL_Q = 9216
COL_K = 10240
N_PROJ = 11264
SMALL_W = 128

SUBLANE = 8
LANES = 128
HALO = SUBLANE
TM_FFN = 1024
TF_FFN = 512
FFN_ROW_CHUNK = 256
FFN_COL_CHUNK = 512
TM_PROJ = 1024
TN_PROJ = 1024
PROJ_ROW_CHUNK = 256
PROJ_COL_CHUNK = 512
TM_OUT = 1024
TN_OUT = 512
TM_XATTN = 512
GLA_BLOCK = SSD_CHUNK
MIX_CHUNKS = 2
MIX_BLOCK = MIX_CHUNKS * SSD_CHUNK
LOG2E = 1.4426950408889634
VMEM_LIMIT = 56 * 1024 * 1024
VMEM_LIMIT_FFN = 62 * 1024 * 1024


def _cparams(sem, vmem_limit=VMEM_LIMIT):
    return pltpu.CompilerParams(dimension_semantics=sem, vmem_limit_bytes=vmem_limit)


def _rms_normalize(x, w):
    ms = jnp.mean(x * x, axis=-1, keepdims=True)
    return x * lax.rsqrt(ms + EPS) * w


def _softplus(x):
    return jnp.maximum(x, 0.0) + jnp.log(1.0 + jnp.exp(-jnp.abs(x)))


def _split_hi_lo(x):
    hi = x.astype(BF16)
    lo = (x - hi.astype(F32)).astype(BF16)
    return hi, lo


def _dot(a, b):
    return jnp.dot(a, b, preferred_element_type=F32)


def _dot_nt(a, b):
    return lax.dot_general(a, b, (((1,), (1,)), ((), ())), preferred_element_type=F32)


def _dot_tn(a, b):
    return lax.dot_general(a, b, (((0,), (0,)), ((), ())), preferred_element_type=F32)


def _ffn_kernel(x_ref, nw_ref, w1_ref, w3_ref, w2_ref, *rest, final):
    if final:
        fnw_ref, o_ref, h_ref = rest
    else:
        o_ref, h_ref = rest
    j = pl.program_id(1)

    tm, d = o_ref.shape

    def for_row_chunks(body):
        def step(r, carry):
            body(pl.ds(pl.multiple_of(r * FFN_ROW_CHUNK, FFN_ROW_CHUNK), FFN_ROW_CHUNK))
            return carry
        lax.fori_loop(0, tm // FFN_ROW_CHUNK, step, 0)

    @pl.when(j == 0)
    def _():
        def body(rs):
            h_ref[rs, :] = _rms_normalize(x_ref[rs, :], nw_ref[...]).astype(BF16)
            o_ref[rs, :] = jnp.zeros((FFN_ROW_CHUNK, d), F32)
        for_row_chunks(body)

    h = h_ref[...]
    g = _dot(h, w1_ref[...])
    u = _dot(h, w3_ref[...])
    a = (g * jax.nn.sigmoid(g) * u).astype(BF16)
    for c in range(0, d, FFN_COL_CHUNK):
        cs = slice(c, c + FFN_COL_CHUNK)
        o_ref[:, cs] += _dot(a, w2_ref[:, cs])

    @pl.when(j == pl.num_programs(1) - 1)
    def _():
        def body(rs):
            y = x_ref[rs, :] + 0.5 * o_ref[rs, :]
            if final:
                y = _rms_normalize(y, fnw_ref[...])
            o_ref[rs, :] = y
        for_row_chunks(body)


def _ffn(x, nw, w1, w3, w2, final_nw=None):
    m, d = x.shape
    dff = w1.shape[1]
    tm, tf = TM_FFN, TF_FFN
    final = final_nw is not None
    in_specs = [
        pl.BlockSpec((tm, d), lambda i, j: (i, 0)),
        pl.BlockSpec((1, d), lambda i, j: (0, 0)),
        pl.BlockSpec((d, tf), lambda i, j: (0, j)),
        pl.BlockSpec((d, tf), lambda i, j: (0, j)),
        pl.BlockSpec((tf, d), lambda i, j: (j, 0)),
    ]
    args = [x, nw, w1, w3, w2]
    if final:
        in_specs.append(pl.BlockSpec((1, d), lambda i, j: (0, 0)))
        args.append(final_nw)
    return pl.pallas_call(
        functools.partial(_ffn_kernel, final=final),
        grid=(m // tm, dff // tf),
        in_specs=in_specs,
        out_specs=pl.BlockSpec((tm, d), lambda i, j: (i, 0)),
        out_shape=jax.ShapeDtypeStruct((m, d), F32),
        scratch_shapes=[pltpu.VMEM((tm, d), BF16)],
        compiler_params=_cparams(("parallel", "arbitrary"), VMEM_LIMIT_FFN),
        name="ffn_final" if final else "ffn",
    )(*args)


def _norm_matmul_kernel(x_ref, nw_ref, w_ref, o_ref, h_ref):
    @pl.when(pl.program_id(1) == 0)
    def _():
        h_ref[...] = _rms_normalize(x_ref[...], nw_ref[...]).astype(BF16)

    o_ref[...] = _dot(h_ref[...], w_ref[...]).astype(o_ref.dtype)


def _norm_matmul(x, nw, w, out_dtype, tm, tn, name):
    m, d = x.shape
    n = w.shape[1]
    return pl.pallas_call(
        _norm_matmul_kernel,
        grid=(m // tm, n // tn),
        in_specs=[
            pl.BlockSpec((tm, d), lambda i, j: (i, 0)),
            pl.BlockSpec((1, d), lambda i, j: (0, 0)),
            pl.BlockSpec((d, tn), lambda i, j: (0, j)),
        ],
        out_specs=pl.BlockSpec((tm, tn), lambda i, j: (i, j)),
        out_shape=jax.ShapeDtypeStruct((m, n), out_dtype),
        scratch_shapes=[pltpu.VMEM((tm, d), BF16)],
        compiler_params=_cparams(("parallel", "arbitrary")),
        name=name,
    )(x, nw, w)


def _in_proj_kernel(x_ref, nw_ref, w_ref, ws_ref, o_ref, small_ref, h_ref):
    tm = h_ref.shape[0]

    @pl.when(pl.program_id(1) == 0)
    def _():
        def step(r, carry):
            rs = pl.ds(pl.multiple_of(r * PROJ_ROW_CHUNK, PROJ_ROW_CHUNK), PROJ_ROW_CHUNK)
            h_ref[rs, :] = _rms_normalize(x_ref[rs, :], nw_ref[...]).astype(BF16)
            return carry
        lax.fori_loop(0, tm // PROJ_ROW_CHUNK, step, 0)
        small_ref[...] = _dot(h_ref[...], ws_ref[...])

    h = h_ref[...]
    for c in range(0, o_ref.shape[1], PROJ_COL_CHUNK):
        cs = slice(c, c + PROJ_COL_CHUNK)
        o_ref[:, cs] = _dot(h, w_ref[:, cs])


def _in_proj(x, nw, w_main, w_small):
    m, d = x.shape
    n = w_main.shape[1]
    tm, tn = TM_PROJ, TN_PROJ
    return pl.pallas_call(
        _in_proj_kernel,
        grid=(m // tm, n // tn),
        in_specs=[
            pl.BlockSpec((tm, d), lambda i, j: (i, 0)),
            pl.BlockSpec((1, d), lambda i, j: (0, 0)),
            pl.BlockSpec((d, tn), lambda i, j: (0, j)),
            pl.BlockSpec((d, SMALL_W), lambda i, j: (0, 0)),
        ],
        out_specs=[
            pl.BlockSpec((tm, tn), lambda i, j: (i, j)),
            pl.BlockSpec((tm, SMALL_W), lambda i, j: (i, 0)),
        ],
        out_shape=[jax.ShapeDtypeStruct((m, n), F32), jax.ShapeDtypeStruct((m, SMALL_W), F32)],
        scratch_shapes=[pltpu.VMEM((tm, d), BF16)],
        compiler_params=_cparams(("parallel", "arbitrary")),
        name="in_proj",
    )(x, nw, w_main, w_small)


def _conv_silu(ext_ref, m_ref, p_ref, n_ref, w_ref, b_ref, has_prev, has_next):
    q = m_ref.shape[0]
    nslab = m_ref.shape[1] // LANES
    for s in range(nslab):
        ls = slice(s * LANES, (s + 1) * LANES)
        ext_ref[s, 0:HALO, :] = jnp.where(has_prev, p_ref[:, ls], 0.0)
        ext_ref[s, HALO:HALO + q, :] = m_ref[:, ls]
        ext_ref[s, HALO + q:HALO + q + HALO, :] = jnp.where(has_next, n_ref[:, ls], 0.0)
    outs = []
    for s in range(nslab):
        ls = slice(s * LANES, (s + 1) * LANES)
        acc = b_ref[:, ls]
        for t in range(CONV_WIDTH):
            r0 = HALO + t - CONV_WIDTH // 2
            acc = acc + ext_ref[s, pl.ds(r0, q, stride=1), :] * w_ref[t:t + 1, ls]
        outs.append(acc * jax.nn.sigmoid(acc))
    return jnp.concatenate(outs, axis=1)


def _ssd_pre(small_ref, rows, dtb_ref, alog_ref, rev):
    q = SSD_CHUNK
    off = SSD_HEADS if rev else 0
    sm_t = small_ref[rows, :].T
    dt_t = _softplus(sm_t[off:off + SSD_HEADS, :] + dtb_ref[...])
    a_t = dt_t * (-jnp.exp(alog_ref[...]))
    r_i = lax.broadcasted_iota(jnp.int32, (q, q), 0)
    c_i = lax.broadcasted_iota(jnp.int32, (q, q), 1)
    tri = (r_i >= c_i) if rev else (r_i <= c_i)
    tri_bf = jnp.where(tri, 1.0, 0.0).astype(BF16)
    a_hi, a_lo = _split_hi_lo(a_t)
    cum_t = _dot(a_hi, tri_bf) + _dot(a_lo, tri_bf)
    tot_t = cum_t[:, 0:1] if rev else cum_t[:, q - 1:q]
    wend_t = dt_t * jnp.exp(tot_t - cum_t)
    dec_t = jnp.exp(cum_t)
    cum2_t = cum_t * LOG2E
    vt = jnp.concatenate([dt_t, wend_t, dec_t, cum2_t], axis=0)
    vv = vt.T
    lane = lax.broadcasted_iota(jnp.int32, (q, LANES), 1)
    return dict(
        vv=vv, vv_bf=vv.astype(BF16), cum2_t=cum2_t,
        causal=(c_i >= r_i) if rev else (c_i <= r_i),
        lo_half=lane < SSD_HEADDIM)


def _ssd_group(g, ctx, xs, bm, cm, e_ref, s_ref, rev):
    q = SSD_CHUNK
    hpg = SSD_HEADS // SSD_GROUPS
    gw = hpg * SSD_HEADDIM
    gs = slice(g * gw, (g + 1) * gw)
    vv, vv_bf, cum2_t = ctx["vv"], ctx["vv_bf"], ctx["cum2_t"]
    xs_g = xs[:, gs]
    xdt = (xs_g * _dot(vv_bf, e_ref[0, :, gs])).astype(BF16)
    xend = (xs_g * _dot(vv_bf, e_ref[1, :, gs])).astype(BF16)
    dec_x = _dot(vv_bf, e_ref[2, :, gs])
    dec_tot = dec_x[0:1, :] if rev else dec_x[q - 1:q, :]
    cm_g = cm[:, g * SSD_STATE:(g + 1) * SSD_STATE]
    bm_g = bm[:, g * SSD_STATE:(g + 1) * SSD_STATE]
    cb = _dot_nt(cm_g, bm_g)
    y_off = _dot(cm_g, s_ref[g].astype(BF16)) * dec_x
    y_pairs = []
    for p in range(hpg // 2):
        ws = []
        for hh in range(2):
            h = g * hpg + 2 * p + hh
            col = 3 * SSD_HEADS + h
            seg = vv[:, col:col + 1] - cum2_t[h:h + 1, :]
            ws.append((cb * jnp.exp2(jnp.where(ctx["causal"], seg, -jnp.inf))).astype(BF16))
        lhs = jnp.concatenate(ws, axis=1)
        xp = xdt[:, p * LANES:(p + 1) * LANES]
        zero = jnp.zeros_like(xp)
        rhs = jnp.concatenate([jnp.where(ctx["lo_half"], xp, zero), jnp.where(ctx["lo_half"], zero, xp)], axis=0)
        y_pairs.append(_dot(lhs, rhs))
    s_ref[g] = s_ref[g] * dec_tot + _dot_tn(bm_g, xend)
    return jnp.concatenate(y_pairs, axis=1) + y_off


def _gla_pre(q_ref, k_ref, v_ref, small_ref, rows, wg_ref, bg_ref, rev):
    tb, qc = GLA_BLOCK, GLA_CHUNK
    r_i = lax.broadcasted_iota(jnp.int32, (tb, tb), 0)
    c_i = lax.broadcasted_iota(jnp.int32, (tb, tb), 1)
    r_blk = r_i // qc
    c_blk = c_i // qc
    order = (c_i >= r_i) if rev else (c_i <= r_i)
    diag_f = jnp.where(r_blk == c_blk, jnp.where(order, 1.0, 0.0), 0.0)
    tri_bf = diag_f.astype(BF16)

    low = small_ref[rows, :].astype(BF16)
    pre = _dot(low, wg_ref[...]) + bg_ref[...]
    gk2 = _softplus(-pre) * (-LOG2E / GLA_NORMALIZER)
    g_hi, g_lo = _split_hi_lo(gk2)
    gg = _dot(tri_bf, g_hi) + _dot(tri_bf, g_lo)

    def rows2(a0, a1):
        n = a0.shape[1]
        return jnp.concatenate([jnp.broadcast_to(a0, (qc, n)), jnp.broadcast_to(a1, (qc, n))], axis=0)

    mid = qc // 2 if rev else qc // 2 - 1
    last = 0 if rev else qc - 1
    gmid = rows2(gg[mid:mid + 1, :], gg[qc + mid:qc + mid + 1, :])
    gl0 = gg[last:last + 1, :]
    gl1 = gg[qc + last:qc + last + 1, :]
    glast = rows2(gl0, gl1)
    dec0 = jnp.exp2(gl0)
    dec1 = jnp.exp2(gl1)
    dec_tot = jnp.exp2(gl0 + gl1)

    qv = q_ref[rows, :] * (GLA_HEAD_K ** -0.5)
    kv = k_ref[rows, :]
    qe = (qv * jnp.exp2(gg - gmid)).astype(BF16)
    ke = (kv * jnp.exp2(gmid - gg)).astype(BF16)
    qin = qv * jnp.exp2(gg)
    kend = kv * jnp.exp2(glast - gg)
    if rev:
        qin_x = jnp.concatenate([qin[:qc] * dec1, qin[qc:]], axis=0)
        kend_x = jnp.concatenate([kend[:qc], kend[qc:] * dec0], axis=0)
    else:
        qin_x = jnp.concatenate([qin[:qc], qin[qc:] * dec0], axis=0)
        kend_x = jnp.concatenate([kend[:qc] * dec1, kend[qc:]], axis=0)
    return dict(
        qe=qe, ke=ke, qin=qin.astype(BF16), kend=kend.astype(BF16),
        qin_x=qin_x.astype(BF16), kend_x=kend_x.astype(BF16),
        vb=v_ref[rows, :].astype(BF16),
        decb=jnp.broadcast_to(dec_tot, (LANES, D_GLA_K)),
        diag=diag_f > 0.5,
        offd=(c_blk - r_blk == 1) if rev else (r_blk - c_blk == 1))


def _gla_head(h, ctx, s_ref):
    ks = slice(h * GLA_HEAD_K, (h + 1) * GLA_HEAD_K)
    vs = slice(h * GLA_HEAD_V, (h + 1) * GLA_HEAD_V)
    a_diag = _dot_nt(ctx["qe"][:, ks], ctx["ke"][:, ks])
    a_off = _dot_nt(ctx["qin"][:, ks], ctx["kend"][:, ks])
    a = jnp.where(ctx["diag"], a_diag, jnp.where(ctx["offd"], a_off, 0.0)).astype(BF16)
    s = s_ref[h]
    vb_h = ctx["vb"][:, vs]
    o_h = _dot(a, vb_h) + _dot(ctx["qin_x"][:, ks], s.astype(BF16))
    upd = _dot_tn(ctx["kend_x"][:, ks], vb_h)
    decb = ctx["decb"]
    dcol = jnp.concatenate(
        [decb[:, h * GLA_HEAD_K + t * LANES:h * GLA_HEAD_K + (t + 1) * LANES].T for t in range(GLA_HEAD_K // LANES)],
        axis=0)
    s_ref[h] = s * jnp.concatenate([dcol] * (GLA_HEAD_V // LANES), axis=1) + upd
    return o_h


def _chunk_position(rev, cps):
    step = pl.program_id(0)
    c = (pl.num_programs(0) - 1 - step) if rev else step
    pos = c % cps
    first = pos == ((cps - 1) if rev else 0)
    return pos, first


def _mix_fwd_kernel(xs_m, xs_p, xs_n, bc_m, bc_p, bc_n, small_ref, q_ref, k_ref, v_ref,
                    cwx_ref, cbx_ref, cwbc_ref, cbbc_ref, dtb_ref, alog_ref, e_ref, wg_ref, bg_ref,
                    yf_ref, xsc_ref, bcc_ref, of_ref,
                    s_ssd, s_gla, ext_x, ext_bc, *, cps):
    pos, first = _chunk_position(False, cps)

    @pl.when(first)
    def _():
        s_ssd[...] = jnp.zeros_like(s_ssd)
        s_gla[...] = jnp.zeros_like(s_gla)

    has_prev = pos != 0
    has_next = pos != cps - 1
    xs = _conv_silu(ext_x, xs_m, xs_p, xs_n, cwx_ref, cbx_ref, has_prev, has_next)
    bc = _conv_silu(ext_bc, bc_m, bc_p, bc_n, cwbc_ref, cbbc_ref, has_prev, has_next).astype(BF16)
    xsc_ref[...] = xs
    bcc_ref[...] = bc
    ng = SSD_GROUPS * SSD_STATE
    gw = D_SSD // SSD_GROUPS
    for sc in range(MIX_CHUNKS):
        rows = slice(sc * SSD_CHUNK, (sc + 1) * SSD_CHUNK)
        sctx = _ssd_pre(small_ref, rows, dtb_ref, alog_ref, False)
        gctx = _gla_pre(q_ref, k_ref, v_ref, small_ref, rows, wg_ref, bg_ref, False)
        for i in range(SSD_GROUPS):
            of_ref[rows, i * GLA_HEAD_V:(i + 1) * GLA_HEAD_V] = _gla_head(i, gctx, s_gla)
            yf_ref[rows, i * gw:(i + 1) * gw] = _ssd_group(
                i, sctx, xs[rows, :], bc[rows, :ng], bc[rows, ng:], e_ref, s_ssd, False)


def _mix_bwd_kernel(xsc_ref, bcc_ref, small_ref, z_ref, yf_ref, q_ref, k_ref, v_ref, gout_ref, of_ref,
                    dtb_ref, alog_ref, e_ref, dskip_ref, snw_ref, wg_ref, bg_ref, gnw_ref,
                    y_ref, o_ref,
                    s_ssd, s_gla, *, cps):
    _, first = _chunk_position(True, cps)

    @pl.when(first)
    def _():
        s_ssd[...] = jnp.zeros_like(s_ssd)
        s_gla[...] = jnp.zeros_like(s_gla)

    ng = SSD_GROUPS * SSD_STATE
    gw = D_SSD // SSD_GROUPS
    for sc in reversed(range(MIX_CHUNKS)):
        rows = slice(sc * SSD_CHUNK, (sc + 1) * SSD_CHUNK)
        sctx = _ssd_pre(small_ref, rows, dtb_ref, alog_ref, True)
        gctx = _gla_pre(q_ref, k_ref, v_ref, small_ref, rows, wg_ref, bg_ref, True)
        xs = xsc_ref[rows, :]
        bc = bcc_ref[rows, :]
        for i in range(SSD_GROUPS):
            vs = slice(i * GLA_HEAD_V, (i + 1) * GLA_HEAD_V)
            o_h = _rms_normalize(_gla_head(i, gctx, s_gla) + of_ref[rows, vs], gnw_ref[...])
            gz = gout_ref[rows, vs]
            o_ref[rows, vs] = (o_h * (gz * jax.nn.sigmoid(gz))).astype(o_ref.dtype)

            gs = slice(i * gw, (i + 1) * gw)
            y = _ssd_group(i, sctx, xs, bc[:, :ng], bc[:, ng:], e_ref, s_ssd, True)
            y = y + yf_ref[rows, gs] + dskip_ref[:, gs] * xs[:, gs]
            zz = z_ref[rows, gs]
            y = y * (zz * jax.nn.sigmoid(zz))
            y_ref[rows, gs] = _rms_normalize(y, snw_ref[:, gs]).astype(y_ref.dtype)


def _const_spec(a):
    nd = a.ndim
    return pl.BlockSpec(a.shape, lambda s: (0,) * nd)


_MIX_SCRATCH = [
    pltpu.VMEM((SSD_GROUPS, SSD_STATE, D_SSD // SSD_GROUPS), F32),
    pltpu.VMEM((GLA_HEADS, GLA_HEAD_K, GLA_HEAD_V), F32),
]


def _mix_fwd(proj, small, prm, cps):
    m = proj.shape[0]
    q = MIX_BLOCK
    nc = m // q
    rb = q // HALO
    last_hb = m // HALO - 1

    def main(colblk):
        return lambda s: (s, colblk)

    def prev(colblk):
        return lambda s: (jnp.maximum(s * rb - 1, 0), colblk)

    def nxt(colblk):
        return lambda s: (jnp.minimum((s + 1) * rb, last_hb), colblk)

    xs_blk, bc_blk = COL_XS // 2048, COL_BC // 1024
    in_specs = [
        pl.BlockSpec((q, 2048), main(xs_blk)),
        pl.BlockSpec((HALO, 2048), prev(xs_blk)),
        pl.BlockSpec((HALO, 2048), nxt(xs_blk)),
        pl.BlockSpec((q, 1024), main(bc_blk)),
        pl.BlockSpec((HALO, 1024), prev(bc_blk)),
        pl.BlockSpec((HALO, 1024), nxt(bc_blk)),
        pl.BlockSpec((q, SMALL_W), main(0)),
        pl.BlockSpec((q, D_GLA_K), main(COL_Q // D_GLA_K)),
        pl.BlockSpec((q, D_GLA_K), main(COL_K // D_GLA_K)),
        pl.BlockSpec((q, D_GLA_V), main(COL_V // D_GLA_V)),
    ]
    params = [prm["cw_x"], prm["cb_x"], prm["cw_bc"], prm["cb_bc"], prm["dtb_f"], prm["alog_f"], prm["expand"],
              prm["gate_w_f"], prm["gate_b_f"]]
    in_specs += [_const_spec(a) for a in params]
    return pl.pallas_call(
        functools.partial(_mix_fwd_kernel, cps=cps),
        grid=(nc,),
        in_specs=in_specs,
        out_specs=[
            pl.BlockSpec((q, D_SSD), main(0)),
            pl.BlockSpec((q, D_SSD), main(0)),
            pl.BlockSpec((q, 2 * SSD_GROUPS * SSD_STATE), main(0)),
            pl.BlockSpec((q, D_GLA_V), main(0)),
        ],
        out_shape=[
            jax.ShapeDtypeStruct((m, D_SSD), F32),
            jax.ShapeDtypeStruct((m, D_SSD), F32),
            jax.ShapeDtypeStruct((m, 2 * SSD_GROUPS * SSD_STATE), BF16),
            jax.ShapeDtypeStruct((m, D_GLA_V), F32),
        ],
        scratch_shapes=_MIX_SCRATCH + [
            pltpu.VMEM((D_SSD // LANES, q + 2 * HALO, LANES), F32),
            pltpu.VMEM((2 * SSD_GROUPS * SSD_STATE // LANES, q + 2 * HALO, LANES), F32),
        ],
        compiler_params=_cparams(("arbitrary",)),
        name="mix_fwd",
    )(*([proj] * 6), small, *([proj] * 3), *params)


def _mix_bwd(proj, small, y_f, xs_c, bc_c, o_f, prm, cps):
    m = proj.shape[0]
    q = MIX_BLOCK
    nc = m // q

    def main(colblk):
        return lambda s: (nc - 1 - s, colblk)

    in_specs = [
        pl.BlockSpec((q, D_SSD), main(0)),
        pl.BlockSpec((q, 2 * SSD_GROUPS * SSD_STATE), main(0)),
        pl.BlockSpec((q, SMALL_W), main(0)),
        pl.BlockSpec((q, 2048), main(COL_Z // 2048)),
        pl.BlockSpec((q, D_SSD), main(0)),
        pl.BlockSpec((q, D_GLA_K), main(COL_Q // D_GLA_K)),
        pl.BlockSpec((q, D_GLA_K), main(COL_K // D_GLA_K)),
        pl.BlockSpec((q, D_GLA_V), main(COL_V // D_GLA_V)),
        pl.BlockSpec((q, D_GLA_V), main(COL_GOUT // D_GLA_V)),
        pl.BlockSpec((q, D_GLA_V), main(0)),
    ]
    params = [prm["dtb_b"], prm["alog_b"], prm["expand"], prm["dskip_x"], prm["ssd_nw"],
              prm["gate_w_b"], prm["gate_b_b"], prm["gla_nw"]]
    in_specs += [_const_spec(a) for a in params]
    return pl.pallas_call(
        functools.partial(_mix_bwd_kernel, cps=cps),
        grid=(nc,),
        in_specs=in_specs,
        out_specs=[pl.BlockSpec((q, D_SSD), main(0)), pl.BlockSpec((q, D_GLA_V), main(0))],
        out_shape=[jax.ShapeDtypeStruct((m, D_SSD), BF16), jax.ShapeDtypeStruct((m, D_GLA_V), BF16)],
        scratch_shapes=_MIX_SCRATCH,
        compiler_params=_cparams(("arbitrary",)),
        name="mix_bwd",
    )(xs_c, bc_c, small, proj, y_f, proj, proj, proj, proj, o_f, *params)


def _out_proj_kernel(x_ref, ya_ref, yb_ref, wa_ref, wb_ref, o_ref):
    o_ref[...] = x_ref[...] + _dot(ya_ref[...], wa_ref[...]) + _dot(yb_ref[...], wb_ref[...])


def _out_proj(x, ya, yb, wa, wb):
    m, d = x.shape
    ka = ya.shape[1]
    kb = yb.shape[1]
    tm, tn = TM_OUT, TN_OUT
    return pl.pallas_call(
        _out_proj_kernel,
        grid=(m // tm, d // tn),
        in_specs=[
            pl.BlockSpec((tm, tn), lambda i, j: (i, j)),
            pl.BlockSpec((tm, ka), lambda i, j: (i, 0)),
            pl.BlockSpec((tm, kb), lambda i, j: (i, 0)),
            pl.BlockSpec((ka, tn), lambda i, j: (0, j)),
            pl.BlockSpec((kb, tn), lambda i, j: (0, j)),
        ],
        out_specs=pl.BlockSpec((tm, tn), lambda i, j: (i, j)),
        out_shape=jax.ShapeDtypeStruct((m, d), F32),
        compiler_params=_cparams(("parallel", "arbitrary")),
        name="out_proj",
    )(x, ya, yb, wa, wb)


def _xattn_kernel(x_ref, nw_ref, wq_ref, kv_ref, wo_ref, o_ref):
    x = x_ref[...]
    h = _rms_normalize(x, nw_ref[...]).astype(BF16)
    qq = _dot(h, wq_ref[...]).astype(BF16)
    scale = XATTN_HEAD_DIM ** -0.5
    heads = []
    for hd in range(XATTN_HEADS):
        ds = slice(hd * XATTN_HEAD_DIM, (hd + 1) * XATTN_HEAD_DIM)
        kh = kv_ref[:, ds]
        vh = kv_ref[:, D_MODEL + hd * XATTN_HEAD_DIM:D_MODEL + (hd + 1) * XATTN_HEAD_DIM]
        s = _dot_nt(qq[:, ds], kh) * scale
        s = s - jnp.max(s, axis=-1, keepdims=True)
        e = jnp.exp(s)
        p = e / jnp.sum(e, axis=-1, keepdims=True)
        heads.append(_dot(p.astype(BF16), vh).astype(BF16))
    o = jnp.concatenate(heads, axis=1)
    o_ref[...] = x + _dot(o, wo_ref[...])


def _xattn(x, nw, wq, kv, wo, tiles_per_seq):
    m, d = x.shape
    tm = TM_XATTN
    single = pl.Buffered(1)
    return pl.pallas_call(
        _xattn_kernel,
        grid=(m // tm,),
        in_specs=[
            pl.BlockSpec((tm, d), lambda i: (i, 0)),
            pl.BlockSpec((1, d), lambda i: (0, 0)),
            pl.BlockSpec((d, d), lambda i: (0, 0), pipeline_mode=single),
            pl.BlockSpec((N_MEM, 2 * d), lambda i: (i // tiles_per_seq, 0)),
            pl.BlockSpec((d, d), lambda i: (0, 0), pipeline_mode=single),
        ],
        out_specs=pl.BlockSpec((tm, d), lambda i: (i, 0)),
        out_shape=jax.ShapeDtypeStruct((m, d), F32),
        compiler_params=_cparams(("arbitrary",)),
        name="xattn",
    )(x, nw, wq, kv, wo)


def _prepare(p):
    w_in = p["w_in"].astype(BF16)
    d = w_in.shape[0]
    pad = jnp.zeros((d, SMALL_W - 2 * SSD_HEADS - 2 * GLA_RANK), BF16)
    w_main = jnp.concatenate([
        w_in[:, 0:2048],
        w_in[:, 2048:4096],
        w_in[:, 7232:9280],
        w_in[:, 9312:11360],
        w_in[:, 4096:5120],
        w_in[:, 5184:6208],
        w_in[:, 6208:7232]],
        axis=1)
    w_small = jnp.concatenate([
        w_in[:, 5120:5184],
        w_in[:, 9280:9312],
        pad], axis=1)

    def gate_w(w, row0):
        full = jnp.zeros((128, D_GLA_K), F32)
        return full.at[row0:row0 + GLA_RANK].set(w).astype(BF16)

    head_of_lane = jnp.arange(D_SSD) // SSD_HEADDIM
    rows = jnp.arange(128)[:, None]
    expand = jnp.stack([(rows == (head_of_lane[None, :] + SSD_HEADS * t)) for t in range(3)], axis=0)
    conv_w, conv_b = p["conv_w"], p["conv_b"]
    return dict(
        w_main=w_main, w_small=w_small,
        cw_x=conv_w[:, :D_SSD], cb_x=conv_b[None, :D_SSD],
        cw_bc=conv_w[:, D_SSD:], cb_bc=conv_b[None, D_SSD:],
        dtb_f=p["dt_bias_fwd"][:, None], dtb_b=p["dt_bias_bwd"][:, None],
        alog_f=p["a_log_fwd"][:, None], alog_b=p["a_log_bwd"][:, None],
        expand=expand.astype(BF16),
        dskip_x=jnp.repeat(p["d_skip"], SSD_HEADDIM)[None, :],
        ssd_nw=p["ssd_norm"][None, :],
        gate_w_f=gate_w(p["gla_gate_w_fwd"], 2 * SSD_HEADS),
        gate_w_b=gate_w(p["gla_gate_w_bwd"], 2 * SSD_HEADS + GLA_RANK),
        gate_b_f=p["gla_gate_b_fwd"][None, :], gate_b_b=p["gla_gate_b_bwd"][None, :],
        gla_nw=p["gla_norm"][None, :],
        w_out_a=p["w_out"][:D_SSD].astype(BF16), w_out_b=p["w_out"][D_SSD:].astype(BF16),
        ffn1=(p["ffn1_norm"][None, :], p["ffn1_w1"].astype(BF16), p["ffn1_w3"].astype(BF16), p["ffn1_w2"].astype(BF16)),
        ffn2=(p["ffn2_norm"][None, :], p["ffn2_w1"].astype(BF16), p["ffn2_w3"].astype(BF16), p["ffn2_w2"].astype(BF16)),
        mix_nw=p["mix_norm"][None, :],
        xattn_nw=p["xattn_norm"][None, :], mem_nw=p["mem_norm"][None, :],
        w_cq=p["w_cq"].astype(BF16), w_ckv=p["w_ckv"].astype(BF16), w_co=p["w_co"].astype(BF16),
        final_nw=p["final_norm"][None, :],
    )


def _trunk(x3, mem3, prm):
    b, l, d = x3.shape
    x = x3.reshape(b * l, d)
    mem = mem3.reshape(b * N_MEM, d)
    x = _ffn(x, *prm["ffn1"])
    proj, small = _in_proj(x, prm["mix_nw"], prm["w_main"], prm["w_small"])
    y_f, xs_c, bc_c, o_f = _mix_fwd(proj, small, prm, l // MIX_BLOCK)
    y, o = _mix_bwd(proj, small, y_f, xs_c, bc_c, o_f, prm, l // MIX_BLOCK)
    x = _out_proj(x, y, o, prm["w_out_a"], prm["w_out_b"])
    kv = _norm_matmul(mem, prm["mem_nw"], prm["w_ckv"], BF16, N_MEM, 1024, "kv_proj")
    x = _xattn(x, prm["xattn_nw"], prm["w_cq"], kv, prm["w_co"], l // TM_XATTN)
    x = _ffn(x, *prm["ffn2"], final_nw=prm["final_nw"])
    return x.reshape(b, l, d)


def kernel(x_prompt, x_sample, mem_prompt, mem_sample, ffn1_norm, ffn1_w1, ffn1_w3, ffn1_w2, mix_norm, w_in, conv_w, conv_b, dt_bias_fwd, dt_bias_bwd, a_log_fwd, a_log_bwd, d_skip, ssd_norm, gla_gate_w_fwd, gla_gate_b_fwd, gla_gate_w_bwd, gla_gate_b_bwd, gla_norm, w_out, xattn_norm, mem_norm, w_cq, w_ckv, w_co, ffn2_norm, ffn2_w1, ffn2_w3, ffn2_w2, final_norm):
    p = dict(
        ffn1_norm=ffn1_norm[0], ffn1_w1=ffn1_w1[0], ffn1_w3=ffn1_w3[0], ffn1_w2=ffn1_w2[0],
        mix_norm=mix_norm[0], w_in=w_in[0], conv_w=conv_w[0], conv_b=conv_b[0],
        dt_bias_fwd=dt_bias_fwd[0], dt_bias_bwd=dt_bias_bwd[0], a_log_fwd=a_log_fwd[0], a_log_bwd=a_log_bwd[0],
        d_skip=d_skip[0], ssd_norm=ssd_norm[0],
        gla_gate_w_fwd=gla_gate_w_fwd[0], gla_gate_b_fwd=gla_gate_b_fwd[0],
        gla_gate_w_bwd=gla_gate_w_bwd[0], gla_gate_b_bwd=gla_gate_b_bwd[0],
        gla_norm=gla_norm[0], w_out=w_out[0], xattn_norm=xattn_norm[0], mem_norm=mem_norm[0],
        w_cq=w_cq[0], w_ckv=w_ckv[0], w_co=w_co[0],
        ffn2_norm=ffn2_norm[0], ffn2_w1=ffn2_w1[0], ffn2_w3=ffn2_w3[0], ffn2_w2=ffn2_w2[0],
        final_norm=final_norm,
    )
    prm = _prepare(p)
    return (_trunk(x_prompt, mem_prompt, prm), _trunk(x_sample, mem_sample, prm))
```

```python
import functools

import jax
import jax.numpy as jnp
from jax import lax
from jax.experimental import pallas as pl
from jax.experimental.pallas import tpu as pltpu

F32 = jnp.float32
BF16 = jnp.bfloat16

D_MODEL = 2048
N_MEM = 256
D_SSD = 2048
SSD_HEADS = 32
SSD_HEADDIM = 64
SSD_GROUPS = 4
SSD_STATE = 128
SSD_CHUNK = 128
CONV_WIDTH = 5
GLA_HEADS = 4
GLA_HEAD_K = 256
GLA_HEAD_V = 512
D_GLA_K = 1024
D_GLA_V = 2048
GLA_RANK = 16
GLA_NORMALIZER = 16.0
GLA_CHUNK = 64
XATTN_HEADS = 4
XATTN_HEAD_DIM = 512
D_FF = 5632
EPS = 1e-6

COL_Z = 0
COL_XS = 2048
COL_V = 4096
COL_GOUT = 6144
COL_BC = 8192
COL_Q = 9216
COL_K = 10240
N_PROJ = 11264
SMALL_W = 128

SUBLANE = 8
LANES = 128
HALO = SUBLANE
TM_FFN = 1024
TF_FFN = 512
FFN_ROW_CHUNK = 256
FFN_COL_CHUNK = 512
TM_PROJ = 1024
TN_PROJ = 1024
PROJ_ROW_CHUNK = 256
PROJ_COL_CHUNK = 512
TM_OUT = 1024
TN_OUT = 512
TM_XATTN = 512
GLA_BLOCK = SSD_CHUNK
MIX_CHUNKS_FWD = 2
MIX_CHUNKS_BWD = 1
LOG2E = 1.4426950408889634
VMEM_LIMIT = 56 * 1024 * 1024
VMEM_LIMIT_FFN = 62 * 1024 * 1024


def _cparams(sem, vmem_limit=VMEM_LIMIT):
    return pltpu.CompilerParams(dimension_semantics=sem, vmem_limit_bytes=vmem_limit)


def _rms_normalize(x, w):
    ms = jnp.mean(x * x, axis=-1, keepdims=True)
    return x * lax.rsqrt(ms + EPS) * w


def _silu(x):
    h = 0.5 * x
    return h + h * jnp.tanh(h)


def _softplus(x):
    return jnp.maximum(x, 0.0) + jnp.log(1.0 + jnp.exp(-jnp.abs(x)))


def _split_hi_lo(x):
    hi = x.astype(BF16)
    lo = (x - hi.astype(F32)).astype(BF16)
    return hi, lo


def _dot(a, b):
    return jnp.dot(a, b, preferred_element_type=F32)


def _dot_nt(a, b):
    return lax.dot_general(a, b, (((1,), (1,)), ((), ())), preferred_element_type=F32)


def _dot_tn(a, b):
    return lax.dot_general(a, b, (((0,), (0,)), ((), ())), preferred_element_type=F32)


def _ffn_kernel(x_ref, nw_ref, w1_ref, w3_ref, w2_ref, *rest, final):
    if final:
        fnw_ref, o_ref, h_ref = rest
    else:
        o_ref, h_ref = rest
    j = pl.program_id(1)

    tm, d = o_ref.shape

    def for_row_chunks(body):
        def step(r, carry):
            body(pl.ds(pl.multiple_of(r * FFN_ROW_CHUNK, FFN_ROW_CHUNK), FFN_ROW_CHUNK))
            return carry
        lax.fori_loop(0, tm // FFN_ROW_CHUNK, step, 0)

    @pl.when(j == 0)
    def _():
        def body(rs):
            h_ref[rs, :] = _rms_normalize(x_ref[rs, :], nw_ref[...]).astype(BF16)
            o_ref[rs, :] = jnp.zeros((FFN_ROW_CHUNK, d), F32)
        for_row_chunks(body)

    h = h_ref[...]
    g = _dot(h, w1_ref[...])
    u = _dot(h, w3_ref[...])
    a = (g * jax.nn.sigmoid(g) * u).astype(BF16)
    for c in range(0, d, FFN_COL_CHUNK):
        cs = slice(c, c + FFN_COL_CHUNK)
        o_ref[:, cs] += _dot(a, w2_ref[:, cs])

    @pl.when(j == pl.num_programs(1) - 1)
    def _():
        def body(rs):
            y = x_ref[rs, :] + 0.5 * o_ref[rs, :]
            if final:
                y = _rms_normalize(y, fnw_ref[...])
            o_ref[rs, :] = y
        for_row_chunks(body)


def _ffn(x, nw, w1, w3, w2, final_nw=None):
    m, d = x.shape
    dff = w1.shape[1]
    tm, tf = TM_FFN, TF_FFN
    final = final_nw is not None
    in_specs = [
        pl.BlockSpec((tm, d), lambda i, j: (i, 0)),
        pl.BlockSpec((1, d), lambda i, j: (0, 0)),
        pl.BlockSpec((d, tf), lambda i, j: (0, j)),
        pl.BlockSpec((d, tf), lambda i, j: (0, j)),
        pl.BlockSpec((tf, d), lambda i, j: (j, 0)),
    ]
    args = [x, nw, w1, w3, w2]
    if final:
        in_specs.append(pl.BlockSpec((1, d), lambda i, j: (0, 0)))
        args.append(final_nw)
    return pl.pallas_call(
        functools.partial(_ffn_kernel, final=final),
        grid=(m // tm, dff // tf),
        in_specs=in_specs,
        out_specs=pl.BlockSpec((tm, d), lambda i, j: (i, 0)),
        out_shape=jax.ShapeDtypeStruct((m, d), F32),
        scratch_shapes=[pltpu.VMEM((tm, d), BF16)],
        compiler_params=_cparams(("parallel", "arbitrary"), VMEM_LIMIT_FFN),
        name="ffn_final" if final else "ffn",
    )(*args)


def _norm_matmul_kernel(x_ref, nw_ref, w_ref, o_ref, h_ref):
    @pl.when(pl.program_id(1) == 0)
    def _():
        h_ref[...] = _rms_normalize(x_ref[...], nw_ref[...]).astype(BF16)

    o_ref[...] = _dot(h_ref[...], w_ref[...]).astype(o_ref.dtype)


def _norm_matmul(x, nw, w, out_dtype, tm, tn, name):
    m, d = x.shape
    n = w.shape[1]
    return pl.pallas_call(
        _norm_matmul_kernel,
        grid=(m // tm, n // tn),
        in_specs=[
            pl.BlockSpec((tm, d), lambda i, j: (i, 0)),
            pl.BlockSpec((1, d), lambda i, j: (0, 0)),
            pl.BlockSpec((d, tn), lambda i, j: (0, j)),
        ],
        out_specs=pl.BlockSpec((tm, tn), lambda i, j: (i, j)),
        out_shape=jax.ShapeDtypeStruct((m, n), out_dtype),
        scratch_shapes=[pltpu.VMEM((tm, d), BF16)],
        compiler_params=_cparams(("parallel", "arbitrary")),
        name=name,
    )(x, nw, w)


def _in_proj_kernel(x_ref, nw_ref, w_ref, ws_ref, o_ref, small_ref, h_ref):
    tm = h_ref.shape[0]

    @pl.when(pl.program_id(1) == 0)
    def _():
        def step(r, carry):
            rs = pl.ds(pl.multiple_of(r * PROJ_ROW_CHUNK, PROJ_ROW_CHUNK), PROJ_ROW_CHUNK)
            h_ref[rs, :] = _rms_normalize(x_ref[rs, :], nw_ref[...]).astype(BF16)
            return carry
        lax.fori_loop(0, tm // PROJ_ROW_CHUNK, step, 0)
        small_ref[...] = _dot(h_ref[...], ws_ref[...])

    h = h_ref[...]
    for c in range(0, o_ref.shape[1], PROJ_COL_CHUNK):
        cs = slice(c, c + PROJ_COL_CHUNK)
        o_ref[:, cs] = _dot(h, w_ref[:, cs])


def _in_proj(x, nw, w_main, w_small):
    m, d = x.shape
    n = w_main.shape[1]
    tm, tn = TM_PROJ, TN_PROJ
    return pl.pallas_call(
        _in_proj_kernel,
        grid=(m // tm, n // tn),
        in_specs=[
            pl.BlockSpec((tm, d), lambda i, j: (i, 0)),
            pl.BlockSpec((1, d), lambda i, j: (0, 0)),
            pl.BlockSpec((d, tn), lambda i, j: (0, j)),
            pl.BlockSpec((d, SMALL_W), lambda i, j: (0, 0)),
        ],
        out_specs=[
            pl.BlockSpec((tm, tn), lambda i, j: (i, j)),
            pl.BlockSpec((tm, SMALL_W), lambda i, j: (i, 0)),
        ],
        out_shape=[jax.ShapeDtypeStruct((m, n), F32), jax.ShapeDtypeStruct((m, SMALL_W), F32)],
        scratch_shapes=[pltpu.VMEM((tm, d), BF16)],
        compiler_params=_cparams(("parallel", "arbitrary")),
        name="in_proj",
    )(x, nw, w_main, w_small)


def _conv_silu(ext_ref, m_ref, p_ref, n_ref, w_ref, b_ref, has_prev, has_next):
    q = m_ref.shape[0]
    nslab = m_ref.shape[1] // LANES
    for s in range(nslab):
        ls = slice(s * LANES, (s + 1) * LANES)
        ext_ref[s, 0:HALO, :] = jnp.where(has_prev, p_ref[:, ls], 0.0)
        ext_ref[s, HALO:HALO + q, :] = m_ref[:, ls]
        ext_ref[s, HALO + q:HALO + q + HALO, :] = jnp.where(has_next, n_ref[:, ls], 0.0)
    outs = []
    for s in range(nslab):
        ls = slice(s * LANES, (s + 1) * LANES)
        acc = b_ref[:, ls]
        for t in range(CONV_WIDTH):
            r0 = HALO + t - CONV_WIDTH // 2
            acc = acc + ext_ref[s, pl.ds(r0, q, stride=1), :] * w_ref[t:t + 1, ls]
        outs.append(_silu(acc))
    return jnp.concatenate(outs, axis=1)


def _ssd_pre(small_ref, rows, dtb_ref, alog_ref, rev):
    q = SSD_CHUNK
    off = SSD_HEADS if rev else 0
    sm_t = small_ref[rows, :].T
    dt_t = _softplus(sm_t[off:off + SSD_HEADS, :] + dtb_ref[...])
    a_t = dt_t * (-jnp.exp(alog_ref[...]))
    r_i = lax.broadcasted_iota(jnp.int32, (q, q), 0)
    c_i = lax.broadcasted_iota(jnp.int32, (q, q), 1)
    tri = (r_i >= c_i) if rev else (r_i <= c_i)
    tri_bf = jnp.where(tri, 1.0, 0.0).astype(BF16)
    a_hi, a_lo = _split_hi_lo(a_t)
    cum_t = _dot(a_hi, tri_bf) + _dot(a_lo, tri_bf)
    tot_t = cum_t[:, 0:1] if rev else cum_t[:, q - 1:q]
    wend_t = dt_t * jnp.exp(tot_t - cum_t)
    dec_t = jnp.exp(cum_t)
    cum2_t = cum_t * LOG2E
    vt = jnp.concatenate([dt_t, wend_t, dec_t, cum2_t], axis=0)
    vv = vt.T
    lane = lax.broadcasted_iota(jnp.int32, (q, LANES), 1)
    return dict(
        vv=vv, vv_bf=vv.astype(BF16), cum2_t=cum2_t,
        causal=(c_i >= r_i) if rev else (c_i <= r_i),
        lo_half=lane < SSD_HEADDIM)


def _ssd_group(g, ctx, xs, bm, cm, e_ref, s_ref, rev):
    q = SSD_CHUNK
    hpg = SSD_HEADS // SSD_GROUPS
    gw = hpg * SSD_HEADDIM
    gs = slice(g * gw, (g + 1) * gw)
    vv, vv_bf, cum2_t = ctx["vv"], ctx["vv_bf"], ctx["cum2_t"]
    xs_g = xs[:, gs]
    xdt = (xs_g * _dot(vv_bf, e_ref[0, :, gs])).astype(BF16)
    xend = (xs_g * _dot(vv_bf, e_ref[1, :, gs])).astype(BF16)
    dec_x = _dot(vv_bf, e_ref[2, :, gs])
    dec_tot = dec_x[0:1, :] if rev else dec_x[q - 1:q, :]
    cm_g = cm[:, g * SSD_STATE:(g + 1) * SSD_STATE]
    bm_g = bm[:, g * SSD_STATE:(g + 1) * SSD_STATE]
    cb = _dot_nt(cm_g, bm_g)
    y_off = _dot(cm_g, s_ref[g].astype(BF16)) * dec_x
    y_pairs = []
    for p in range(hpg // 2):
        ws = []
        for hh in range(2):
            h = g * hpg + 2 * p + hh
            col = 3 * SSD_HEADS + h
            seg = vv[:, col:col + 1] - cum2_t[h:h + 1, :]
            ws.append((cb * jnp.exp2(jnp.where(ctx["causal"], seg, -jnp.inf))).astype(BF16))
        lhs = jnp.concatenate(ws, axis=1)
        xp = xdt[:, p * LANES:(p + 1) * LANES]
        zero = jnp.zeros_like(xp)
        rhs = jnp.concatenate([jnp.where(ctx["lo_half"], xp, zero), jnp.where(ctx["lo_half"], zero, xp)], axis=0)
        y_pairs.append(_dot(lhs, rhs))
    s_ref[g] = s_ref[g] * dec_tot + _dot_tn(bm_g, xend)
    return jnp.concatenate(y_pairs, axis=1) + y_off


def _gla_pre(q_ref, k_ref, v_ref, small_ref, rows, wg_ref, bg_ref, rev):
    tb, qc = GLA_BLOCK, GLA_CHUNK
    r_i = lax.broadcasted_iota(jnp.int32, (tb, tb), 0)
    c_i = lax.broadcasted_iota(jnp.int32, (tb, tb), 1)
    r_blk = r_i // qc
    c_blk = c_i // qc
    order = (c_i >= r_i) if rev else (c_i <= r_i)
    diag_f = jnp.where(r_blk == c_blk, jnp.where(order, 1.0, 0.0), 0.0)
    tri_bf = diag_f.astype(BF16)

    low = small_ref[rows, :].astype(BF16)
    pre = _dot(low, wg_ref[...]) + bg_ref[...]
    gk2 = _softplus(-pre) * (-LOG2E / GLA_NORMALIZER)
    g_hi, g_lo = _split_hi_lo(gk2)
    gg = _dot(tri_bf, g_hi) + _dot(tri_bf, g_lo)

    def rows2(a0, a1):
        n = a0.shape[1]
        return jnp.concatenate([jnp.broadcast_to(a0, (qc, n)), jnp.broadcast_to(a1, (qc, n))], axis=0)

    mid = qc // 2 if rev else qc // 2 - 1
    last = 0 if rev else qc - 1
    gmid = rows2(gg[mid:mid + 1, :], gg[qc + mid:qc + mid + 1, :])
    gl0 = gg[last:last + 1, :]
    gl1 = gg[qc + last:qc + last + 1, :]
    glast = rows2(gl0, gl1)
    dec0 = jnp.exp2(gl0)
    dec1 = jnp.exp2(gl1)
    dec_tot = jnp.exp2(gl0 + gl1)

    qe_f = q_ref[rows, :] * jnp.exp2(gg - (gmid + 0.5 * jnp.log2(float(GLA_HEAD_K))))
    ke_f = k_ref[rows, :] * jnp.exp2(gmid - gg)
    qe = qe_f.astype(BF16)
    ke = ke_f.astype(BF16)
    qin = qe_f * jnp.exp2(gmid)
    kend = ke_f * jnp.exp2(glast - gmid)
    if rev:
        qin_x = jnp.concatenate([qin[:qc] * dec1, qin[qc:]], axis=0)
        kend_x = jnp.concatenate([kend[:qc], kend[qc:] * dec0], axis=0)
    else:
        qin_x = jnp.concatenate([qin[:qc], qin[qc:] * dec0], axis=0)
        kend_x = jnp.concatenate([kend[:qc] * dec1, kend[qc:]], axis=0)
    return dict(
        qe=qe, ke=ke, qin=qin.astype(BF16), kend=kend.astype(BF16),
        qin_x=qin_x.astype(BF16), kend_x=kend_x.astype(BF16),
        vb=v_ref[rows, :].astype(BF16),
        decb=jnp.broadcast_to(dec_tot, (LANES, D_GLA_K)),
        diag=diag_f > 0.5,
        offd=(c_blk - r_blk == 1) if rev else (r_blk - c_blk == 1))


def _gla_head(h, ctx, s_ref):
    ks = slice(h * GLA_HEAD_K, (h + 1) * GLA_HEAD_K)
    vs = slice(h * GLA_HEAD_V, (h + 1) * GLA_HEAD_V)
    a_diag = _dot_nt(ctx["qe"][:, ks], ctx["ke"][:, ks])
    a_off = _dot_nt(ctx["qin"][:, ks], ctx["kend"][:, ks])
    a = jnp.where(ctx["diag"], a_diag, jnp.where(ctx["offd"], a_off, 0.0)).astype(BF16)
    s = s_ref[h]
    vb_h = ctx["vb"][:, vs]
    o_h = _dot(a, vb_h) + _dot(ctx["qin_x"][:, ks], s.astype(BF16))
    upd = _dot_tn(ctx["kend_x"][:, ks], vb_h)
    decb = ctx["decb"]
    dcol = jnp.concatenate(
        [decb[:, h * GLA_HEAD_K + t * LANES:h * GLA_HEAD_K + (t + 1) * LANES].T for t in range(GLA_HEAD_K // LANES)],
        axis=0)
    s_ref[h] = s * jnp.concatenate([dcol] * (GLA_HEAD_V // LANES), axis=1) + upd
    return o_h


def _chunk_position(rev, cps):
    step = pl.program_id(0)
    c = (pl.num_programs(0) - 1 - step) if rev else step
    pos = c % cps
    first = pos == ((cps - 1) if rev else 0)
    return pos, first


def _mix_fwd_kernel(xs_m, xs_p, xs_n, bc_m, bc_p, bc_n, small_ref, q_ref, k_ref, v_ref,
                    cwx_ref, cbx_ref, cwbc_ref, cbbc_ref, dtb_ref, alog_ref, e_ref, wg_ref, bg_ref,
                    yf_ref, xsc_ref, bcc_ref, of_ref,
                    s_ssd, s_gla, ext_x, ext_bc, *, cps):
    pos, first = _chunk_position(False, cps)

    @pl.when(first)
    def _():
        s_ssd[...] = jnp.zeros_like(s_ssd)
        s_gla[...] = jnp.zeros_like(s_gla)

    has_prev = pos != 0
    has_next = pos != cps - 1
    xs = _conv_silu(ext_x, xs_m, xs_p, xs_n, cwx_ref, cbx_ref, has_prev, has_next)
    bc = _conv_silu(ext_bc, bc_m, bc_p, bc_n, cwbc_ref, cbbc_ref, has_prev, has_next).astype(BF16)
    xsc_ref[...] = xs
    bcc_ref[...] = bc
    ng = SSD_GROUPS * SSD_STATE
    gw = D_SSD // SSD_GROUPS
    for sc in range(MIX_CHUNKS_FWD):
        rows = slice(sc * SSD_CHUNK, (sc + 1) * SSD_CHUNK)
        sctx = _ssd_pre(small_ref, rows, dtb_ref, alog_ref, False)
        gctx = _gla_pre(q_ref, k_ref, v_ref, small_ref, rows, wg_ref, bg_ref, False)
        for i in range(SSD_GROUPS):
            of_ref[rows, i * GLA_HEAD_V:(i + 1) * GLA_HEAD_V] = _gla_head(i, gctx, s_gla)
            yf_ref[rows, i * gw:(i + 1) * gw] = _ssd_group(
                i, sctx, xs[rows, :], bc[rows, :ng], bc[rows, ng:], e_ref, s_ssd, False)


def _mix_bwd_kernel(xsc_ref, bcc_ref, small_ref, z_ref, yf_ref, q_ref, k_ref, v_ref, gout_ref, of_ref,
                    dtb_ref, alog_ref, e_ref, dskip_ref, snw_ref, wg_ref, bg_ref, gnw_ref,
                    y_ref, o_ref,
                    s_ssd, s_gla, *, cps):
    _, first = _chunk_position(True, cps)

    @pl.when(first)
    def _():
        s_ssd[...] = jnp.zeros_like(s_ssd)
        s_gla[...] = jnp.zeros_like(s_gla)

    ng = SSD_GROUPS * SSD_STATE
    gw = D_SSD // SSD_GROUPS
    for sc in reversed(range(MIX_CHUNKS_BWD)):
        rows = slice(sc * SSD_CHUNK, (sc + 1) * SSD_CHUNK)
        sctx = _ssd_pre(small_ref, rows, dtb_ref, alog_ref, True)
        gctx = _gla_pre(q_ref, k_ref, v_ref, small_ref, rows, wg_ref, bg_ref, True)
        xs = xsc_ref[rows, :]
        bc = bcc_ref[rows, :]
        for i in range(SSD_GROUPS):
            vs = slice(i * GLA_HEAD_V, (i + 1) * GLA_HEAD_V)
            o_h = _rms_normalize(_gla_head(i, gctx, s_gla) + of_ref[rows, vs], gnw_ref[...])
            gz = gout_ref[rows, vs]
            o_ref[rows, vs] = (o_h * _silu(gz)).astype(o_ref.dtype)

            gs = slice(i * gw, (i + 1) * gw)
            y = _ssd_group(i, sctx, xs, bc[:, :ng], bc[:, ng:], e_ref, s_ssd, True)
            y = y + yf_ref[rows, gs] + dskip_ref[:, gs] * xs[:, gs]
            zz = z_ref[rows, gs]
            y = y * _silu(zz)
            y_ref[rows, gs] = _rms_normalize(y, snw_ref[:, gs]).astype(y_ref.dtype)


def _const_spec(a):
    nd = a.ndim
    return pl.BlockSpec(a.shape, lambda s: (0,) * nd)


_MIX_SCRATCH = [
    pltpu.VMEM((SSD_GROUPS, SSD_STATE, D_SSD // SSD_GROUPS), F32),
    pltpu.VMEM((GLA_HEADS, GLA_HEAD_K, GLA_HEAD_V), F32),
]


def _mix_fwd(proj, small, prm, cps):
    m = proj.shape[0]
    q = MIX_CHUNKS_FWD * SSD_CHUNK
    nc = m // q
    rb = q // HALO
    last_hb = m // HALO - 1

    def main(colblk):
        return lambda s: (s, colblk)

    def prev(colblk):
        return lambda s: (jnp.maximum(s * rb - 1, 0), colblk)

    def nxt(colblk):
        return lambda s: (jnp.minimum((s + 1) * rb, last_hb), colblk)

    xs_blk, bc_blk = COL_XS // 2048, COL_BC // 1024
    in_specs = [
        pl.BlockSpec((q, 2048), main(xs_blk)),
        pl.BlockSpec((HALO, 2048), prev(xs_blk)),
        pl.BlockSpec((HALO, 2048), nxt(xs_blk)),
        pl.BlockSpec((q, 1024), main(bc_blk)),
        pl.BlockSpec((HALO, 1024), prev(bc_blk)),
        pl.BlockSpec((HALO, 1024), nxt(bc_blk)),
        pl.BlockSpec((q, SMALL_W), main(0)),
        pl.BlockSpec((q, D_GLA_K), main(COL_Q // D_GLA_K)),
        pl.BlockSpec((q, D_GLA_K), main(COL_K // D_GLA_K)),
        pl.BlockSpec((q, D_GLA_V), main(COL_V // D_GLA_V)),
    ]
    params = [prm["cw_x"], prm["cb_x"], prm["cw_bc"], prm["cb_bc"], prm["dtb_f"], prm["alog_f"], prm["expand"],
              prm["gate_w_f"], prm["gate_b_f"]]
    in_specs += [_const_spec(a) for a in params]
    return pl.pallas_call(
        functools.partial(_mix_fwd_kernel, cps=cps),
        grid=(nc,),
        in_specs=in_specs,
        out_specs=[
            pl.BlockSpec((q, D_SSD), main(0)),
            pl.BlockSpec((q, D_SSD), main(0)),
            pl.BlockSpec((q, 2 * SSD_GROUPS * SSD_STATE), main(0)),
            pl.BlockSpec((q, D_GLA_V), main(0)),
        ],
        out_shape=[
            jax.ShapeDtypeStruct((m, D_SSD), F32),
            jax.ShapeDtypeStruct((m, D_SSD), F32),
            jax.ShapeDtypeStruct((m, 2 * SSD_GROUPS * SSD_STATE), BF16),
            jax.ShapeDtypeStruct((m, D_GLA_V), F32),
        ],
        scratch_shapes=_MIX_SCRATCH + [
            pltpu.VMEM((D_SSD // LANES, q + 2 * HALO, LANES), F32),
            pltpu.VMEM((2 * SSD_GROUPS * SSD_STATE // LANES, q + 2 * HALO, LANES), F32),
        ],
        compiler_params=_cparams(("arbitrary",)),
        name="mix_fwd",
    )(*([proj] * 6), small, *([proj] * 3), *params)


def _mix_bwd(proj, small, y_f, xs_c, bc_c, o_f, prm, cps):
    m = proj.shape[0]
    q = MIX_CHUNKS_BWD * SSD_CHUNK
    nc = m // q

    def main(colblk):
        return lambda s: (nc - 1 - s, colblk)

    in_specs = [
        pl.BlockSpec((q, D_SSD), main(0)),
        pl.BlockSpec((q, 2 * SSD_GROUPS * SSD_STATE), main(0)),
        pl.BlockSpec((q, SMALL_W), main(0)),
        pl.BlockSpec((q, 2048), main(COL_Z // 2048)),
        pl.BlockSpec((q, D_SSD), main(0)),
        pl.BlockSpec((q, D_GLA_K), main(COL_Q // D_GLA_K)),
        pl.BlockSpec((q, D_GLA_K), main(COL_K // D_GLA_K)),
        pl.BlockSpec((q, D_GLA_V), main(COL_V // D_GLA_V)),
        pl.BlockSpec((q, D_GLA_V), main(COL_GOUT // D_GLA_V)),
        pl.BlockSpec((q, D_GLA_V), main(0)),
    ]
    params = [prm["dtb_b"], prm["alog_b"], prm["expand"], prm["dskip_x"], prm["ssd_nw"],
              prm["gate_w_b"], prm["gate_b_b"], prm["gla_nw"]]
    in_specs += [_const_spec(a) for a in params]
    return pl.pallas_call(
        functools.partial(_mix_bwd_kernel, cps=cps),
        grid=(nc,),
        in_specs=in_specs,
        out_specs=[pl.BlockSpec((q, D_SSD), main(0)), pl.BlockSpec((q, D_GLA_V), main(0))],
        out_shape=[jax.ShapeDtypeStruct((m, D_SSD), BF16), jax.ShapeDtypeStruct((m, D_GLA_V), BF16)],
        scratch_shapes=_MIX_SCRATCH,
        compiler_params=_cparams(("arbitrary",)),
        name="mix_bwd",
    )(xs_c, bc_c, small, proj, y_f, proj, proj, proj, proj, o_f, *params)


def _out_proj_kernel(x_ref, ya_ref, yb_ref, wa_ref, wb_ref, o_ref):
    o_ref[...] = x_ref[...] + _dot(ya_ref[...], wa_ref[...]) + _dot(yb_ref[...], wb_ref[...])


def _out_proj(x, ya, yb, wa, wb):
    m, d = x.shape
    ka = ya.shape[1]
    kb = yb.shape[1]
    tm, tn = TM_OUT, TN_OUT
    return pl.pallas_call(
        _out_proj_kernel,
        grid=(m // tm, d // tn),
        in_specs=[
            pl.BlockSpec((tm, tn), lambda i, j: (i, j)),
            pl.BlockSpec((tm, ka), lambda i, j: (i, 0)),
            pl.BlockSpec((tm, kb), lambda i, j: (i, 0)),
            pl.BlockSpec((ka, tn), lambda i, j: (0, j)),
            pl.BlockSpec((kb, tn), lambda i, j: (0, j)),
        ],
        out_specs=pl.BlockSpec((tm, tn), lambda i, j: (i, j)),
        out_shape=jax.ShapeDtypeStruct((m, d), F32),
        compiler_params=_cparams(("parallel", "arbitrary")),
        name="out_proj",
    )(x, ya, yb, wa, wb)


def _xattn_kernel(x_ref, nw_ref, wq_ref, kv_ref, wo_ref, o_ref):
    x = x_ref[...]
    h = _rms_normalize(x, nw_ref[...]).astype(BF16)
    qq = _dot(h, wq_ref[...]).astype(BF16)
    scale = XATTN_HEAD_DIM ** -0.5
    heads = []
    for hd in range(XATTN_HEADS):
        ds = slice(hd * XATTN_HEAD_DIM, (hd + 1) * XATTN_HEAD_DIM)
        kh = kv_ref[:, ds]
        vh = kv_ref[:, D_MODEL + hd * XATTN_HEAD_DIM:D_MODEL + (hd + 1) * XATTN_HEAD_DIM]
        s = _dot_nt(qq[:, ds], kh) * scale
        s = s - jnp.max(s, axis=-1, keepdims=True)
        e = jnp.exp(s)
        p = e / jnp.sum(e, axis=-1, keepdims=True)
        heads.append(_dot(p.astype(BF16), vh).astype(BF16))
    o = jnp.concatenate(heads, axis=1)
    o_ref[...] = x + _dot(o, wo_ref[...])


def _xattn(x, nw, wq, kv, wo, tiles_per_seq):
    m, d = x.shape
    tm = TM_XATTN
    single = pl.Buffered(1)
    return pl.pallas_call(
        _xattn_kernel,
        grid=(m // tm,),
        in_specs=[
            pl.BlockSpec((tm, d), lambda i: (i, 0)),
            pl.BlockSpec((1, d), lambda i: (0, 0)),
            pl.BlockSpec((d, d), lambda i: (0, 0), pipeline_mode=single),
            pl.BlockSpec((N_MEM, 2 * d), lambda i: (i // tiles_per_seq, 0)),
            pl.BlockSpec((d, d), lambda i: (0, 0), pipeline_mode=single),
        ],
        out_specs=pl.BlockSpec((tm, d), lambda i: (i, 0)),
        out_shape=jax.ShapeDtypeStruct((m, d), F32),
        compiler_params=_cparams(("arbitrary",)),
        name="xattn",
    )(x, nw, wq, kv, wo)


def _prepare(p):
    w_in = p["w_in"].astype(BF16)
    d = w_in.shape[0]
    pad = jnp.zeros((d, SMALL_W - 2 * SSD_HEADS - 2 * GLA_RANK), BF16)
    w_main = jnp.concatenate([
        w_in[:, 0:2048],
        w_in[:, 2048:4096],
        w_in[:, 7232:9280],
        w_in[:, 9312:11360],
        w_in[:, 4096:5120],
        w_in[:, 5184:6208],
        w_in[:, 6208:7232]],
        axis=1)
    w_small = jnp.concatenate([
        w_in[:, 5120:5184],
        w_in[:, 9280:9312],
        pad], axis=1)

    def gate_w(w, row0):
        full = jnp.zeros((128, D_GLA_K), F32)
        return full.at[row0:row0 + GLA_RANK].set(w).astype(BF16)

    head_of_lane = jnp.arange(D_SSD) // SSD_HEADDIM
    rows = jnp.arange(128)[:, None]
    expand = jnp.stack([(rows == (head_of_lane[None, :] + SSD_HEADS * t)) for t in range(3)], axis=0)
    conv_w, conv_b = p["conv_w"], p["conv_b"]
    return dict(
        w_main=w_main, w_small=w_small,
        cw_x=conv_w[:, :D_SSD], cb_x=conv_b[None, :D_SSD],
        cw_bc=conv_w[:, D_SSD:], cb_bc=conv_b[None, D_SSD:],
        dtb_f=p["dt_bias_fwd"][:, None], dtb_b=p["dt_bias_bwd"][:, None],
        alog_f=p["a_log_fwd"][:, None], alog_b=p["a_log_bwd"][:, None],
        expand=expand.astype(BF16),
        dskip_x=jnp.repeat(p["d_skip"], SSD_HEADDIM)[None, :],
        ssd_nw=p["ssd_norm"][None, :],
        gate_w_f=gate_w(p["gla_gate_w_fwd"], 2 * SSD_HEADS),
        gate_w_b=gate_w(p["gla_gate_w_bwd"], 2 * SSD_HEADS + GLA_RANK),
        gate_b_f=p["gla_gate_b_fwd"][None, :], gate_b_b=p["gla_gate_b_bwd"][None, :],
        gla_nw=p["gla_norm"][None, :],
        w_out_a=p["w_out"][:D_SSD].astype(BF16), w_out_b=p["w_out"][D_SSD:].astype(BF16),
        ffn1=(p["ffn1_norm"][None, :], p["ffn1_w1"].astype(BF16), p["ffn1_w3"].astype(BF16), p["ffn1_w2"].astype(BF16)),
        ffn2=(p["ffn2_norm"][None, :], p["ffn2_w1"].astype(BF16), p["ffn2_w3"].astype(BF16), p["ffn2_w2"].astype(BF16)),
        mix_nw=p["mix_norm"][None, :],
        xattn_nw=p["xattn_norm"][None, :], mem_nw=p["mem_norm"][None, :],
        w_cq=p["w_cq"].astype(BF16), w_ckv=p["w_ckv"].astype(BF16), w_co=p["w_co"].astype(BF16),
        final_nw=p["final_norm"][None, :],
    )


def _trunk(x3, mem3, prm):
    b, l, d = x3.shape
    x = x3.reshape(b * l, d)
    mem = mem3.reshape(b * N_MEM, d)
    x = _ffn(x, *prm["ffn1"])
    proj, small = _in_proj(x, prm["mix_nw"], prm["w_main"], prm["w_small"])
    y_f, xs_c, bc_c, o_f = _mix_fwd(proj, small, prm, l // (MIX_CHUNKS_FWD * SSD_CHUNK))
    y, o = _mix_bwd(proj, small, y_f, xs_c, bc_c, o_f, prm, l // (MIX_CHUNKS_BWD * SSD_CHUNK))
    x = _out_proj(x, y, o, prm["w_out_a"], prm["w_out_b"])
    kv = _norm_matmul(mem, prm["mem_nw"], prm["w_ckv"], BF16, N_MEM, 1024, "kv_proj")
    x = _xattn(x, prm["xattn_nw"], prm["w_cq"], kv, prm["w_co"], l // TM_XATTN)
    x = _ffn(x, *prm["ffn2"], final_nw=prm["final_nw"])
    return x.reshape(b, l, d)


def kernel(x_prompt, x_sample, mem_prompt, mem_sample, ffn1_norm, ffn1_w1, ffn1_w3, ffn1_w2, mix_norm, w_in, conv_w, conv_b, dt_bias_fwd, dt_bias_bwd, a_log_fwd, a_log_bwd, d_skip, ssd_norm, gla_gate_w_fwd, gla_gate_b_fwd, gla_gate_w_bwd, gla_gate_b_bwd, gla_norm, w_out, xattn_norm, mem_norm, w_cq, w_ckv, w_co, ffn2_norm, ffn2_w1, ffn2_w3, ffn2_w2, final_norm):
    p = dict(
        ffn1_norm=ffn1_norm[0], ffn1_w1=ffn1_w1[0], ffn1_w3=ffn1_w3[0], ffn1_w2=ffn1_w2[0],
        mix_norm=mix_norm[0], w_in=w_in[0], conv_w=conv_w[0], conv_b=conv_b[0],
        dt_bias_fwd=dt_bias_fwd[0], dt_bias_bwd=dt_bias_bwd[0], a_log_fwd=a_log_fwd[0], a_log_bwd=a_log_bwd[0],
        d_skip=d_skip[0], ssd_norm=ssd_norm[0],
        gla_gate_w_fwd=gla_gate_w_fwd[0], gla_gate_b_fwd=gla_gate_b_fwd[0],
        gla_gate_w_bwd=gla_gate_w_bwd[0], gla_gate_b_bwd=gla_gate_b_bwd[0],
        gla_norm=gla_norm[0], w_out=w_out[0], xattn_norm=xattn_norm[0], mem_norm=mem_norm[0],
        w_cq=w_cq[0], w_ckv=w_ckv[0], w_co=w_co[0],
        ffn2_norm=ffn2_norm[0], ffn2_w1=ffn2_w1[0], ffn2_w3=ffn2_w3[0], ffn2_w2=ffn2_w2[0],
        final_norm=final_norm,
    )
    prm = _prepare(p)
    return (_trunk(x_prompt, mem_prompt, prm), _trunk(x_sample, mem_sample, prm))
```

```python
import functools

import jax
import jax.numpy as jnp
from jax import lax
from jax.experimental import pallas as pl
from jax.experimental.pallas import tpu as pltpu

F32 = jnp.float32
BF16 = jnp.bfloat16

D_MODEL = 2048
N_MEM = 256
D_SSD = 2048
SSD_HEADS = 32
SSD_HEADDIM = 64
SSD_GROUPS = 4
SSD_STATE = 128
SSD_CHUNK = 128
CONV_WIDTH = 5
GLA_HEADS = 4
GLA_HEAD_K = 256
GLA_HEAD_V = 512
D_GLA_K = 1024
D_GLA_V = 2048
GLA_RANK = 16
GLA_NORMALIZER = 16.0
GLA_CHUNK = 64
XATTN_HEADS = 4
XATTN_HEAD_DIM = 512
D_FF = 5632
EPS = 1e-6

COL_Z = 0
COL_XS = 2048
COL_V = 4096
COL_GOUT = 6144
COL_BC = 8192
COL_Q = 9216
COL_K = 10240
N_PROJ = 11264
SMALL_W = 128

SUBLANE = 8
LANES = 128
HALO = SUBLANE
TM_FFN = 1024
TF_FFN = 512
FFN_ROW_CHUNK = 256
FFN_COL_CHUNK = 512
TM_PROJ = 1024
TN_PROJ = 1024
PROJ_ROW_CHUNK = 256
PROJ_COL_CHUNK = 512
TM_OUT = 1024
TN_OUT = 512
TM_XATTN = 512
GLA_BLOCK = SSD_CHUNK
MIX_CHUNKS_FWD = 2
MIX_CHUNKS_BWD = 1
LOG2E = 1.4426950408889634
VMEM_LIMIT = 56 * 1024 * 1024
VMEM_LIMIT_FFN = 62 * 1024 * 1024


def _cparams(sem, vmem_limit=VMEM_LIMIT):
    return pltpu.CompilerParams(dimension_semantics=sem, vmem_limit_bytes=vmem_limit)


def _rms_normalize(x, w):
    ms = jnp.mean(x * x, axis=-1, keepdims=True)
    return x * lax.rsqrt(ms + EPS) * w


def _silu(x):
    h = 0.5 * x
    return h + h * jnp.tanh(h)


def _softplus(x):
    return jnp.maximum(x, 0.0) + jnp.log(1.0 + jnp.exp(-jnp.abs(x)))


def _split_hi_lo(x):
    hi = x.astype(BF16)
    lo = (x - hi.astype(F32)).astype(BF16)
    return hi, lo


def _dot(a, b):
    return jnp.dot(a, b, preferred_element_type=F32)


def _dot_nt(a, b):
    return lax.dot_general(a, b, (((1,), (1,)), ((), ())), preferred_element_type=F32)


def _dot_tn(a, b):
    return lax.dot_general(a, b, (((0,), (0,)), ((), ())), preferred_element_type=F32)


def _ffn_kernel(x_ref, nw_ref, w1_ref, w3_ref, w2_ref, *rest, final):
    if final:
        fnw_ref, o_ref, h_ref = rest
    else:
        o_ref, h_ref = rest
    j = pl.program_id(1)

    tm, d = o_ref.shape

    def for_row_chunks(body):
        def step(r, carry):
            body(pl.ds(pl.multiple_of(r * FFN_ROW_CHUNK, FFN_ROW_CHUNK), FFN_ROW_CHUNK))
            return carry
        lax.fori_loop(0, tm // FFN_ROW_CHUNK, step, 0)

    @pl.when(j == 0)
    def _():
        def body(rs):
            h_ref[rs, :] = _rms_normalize(x_ref[rs, :], nw_ref[...]).astype(BF16)
            o_ref[rs, :] = jnp.zeros((FFN_ROW_CHUNK, d), F32)
        for_row_chunks(body)

    h = h_ref[...]
    g = _dot(h, w1_ref[...])
    u = _dot(h, w3_ref[...])
    a = (g * jax.nn.sigmoid(g) * u).astype(BF16)
    for c in range(0, d, FFN_COL_CHUNK):
        cs = slice(c, c + FFN_COL_CHUNK)
        o_ref[:, cs] += _dot(a, w2_ref[:, cs])

    @pl.when(j == pl.num_programs(1) - 1)
    def _():
        def body(rs):
            y = x_ref[rs, :] + 0.5 * o_ref[rs, :]
            if final:
                y = _rms_normalize(y, fnw_ref[...])
            o_ref[rs, :] = y
        for_row_chunks(body)


def _ffn(x, nw, w1, w3, w2, final_nw=None):
    m, d = x.shape
    dff = w1.shape[1]
    tm, tf = TM_FFN, TF_FFN
    final = final_nw is not None
    in_specs = [
        pl.BlockSpec((tm, d), lambda i, j: (i, 0)),
        pl.BlockSpec((1, d), lambda i, j: (0, 0)),
        pl.BlockSpec((d, tf), lambda i, j: (0, j)),
        pl.BlockSpec((d, tf), lambda i, j: (0, j)),
        pl.BlockSpec((tf, d), lambda i, j: (j, 0)),
    ]
    args = [x, nw, w1, w3, w2]
    if final:
        in_specs.append(pl.BlockSpec((1, d), lambda i, j: (0, 0)))
        args.append(final_nw)
    return pl.pallas_call(
        functools.partial(_ffn_kernel, final=final),
        grid=(m // tm, dff // tf),
        in_specs=in_specs,
        out_specs=pl.BlockSpec((tm, d), lambda i, j: (i, 0)),
        out_shape=jax.ShapeDtypeStruct((m, d), F32),
        scratch_shapes=[pltpu.VMEM((tm, d), BF16)],
        compiler_params=_cparams(("parallel", "arbitrary"), VMEM_LIMIT_FFN),
        name="ffn_final" if final else "ffn",
    )(*args)


def _norm_matmul_kernel(x_ref, nw_ref, w_ref, o_ref, h_ref):
    @pl.when(pl.program_id(1) == 0)
    def _():
        h_ref[...] = _rms_normalize(x_ref[...], nw_ref[...]).astype(BF16)

    o_ref[...] = _dot(h_ref[...], w_ref[...]).astype(o_ref.dtype)


def _norm_matmul(x, nw, w, out_dtype, tm, tn, name):
    m, d = x.shape
    n = w.shape[1]
    return pl.pallas_call(
        _norm_matmul_kernel,
        grid=(m // tm, n // tn),
        in_specs=[
            pl.BlockSpec((tm, d), lambda i, j: (i, 0)),
            pl.BlockSpec((1, d), lambda i, j: (0, 0)),
            pl.BlockSpec((d, tn), lambda i, j: (0, j)),
        ],
        out_specs=pl.BlockSpec((tm, tn), lambda i, j: (i, j)),
        out_shape=jax.ShapeDtypeStruct((m, n), out_dtype),
        scratch_shapes=[pltpu.VMEM((tm, d), BF16)],
        compiler_params=_cparams(("parallel", "arbitrary")),
        name=name,
    )(x, nw, w)


def _in_proj_kernel(x_ref, nw_ref, w_ref, ws_ref, o_ref, small_ref, h_ref):
    tm = h_ref.shape[0]

    @pl.when(pl.program_id(1) == 0)
    def _():
        def step(r, carry):
            rs = pl.ds(pl.multiple_of(r * PROJ_ROW_CHUNK, PROJ_ROW_CHUNK), PROJ_ROW_CHUNK)
            h_ref[rs, :] = _rms_normalize(x_ref[rs, :], nw_ref[...]).astype(BF16)
            return carry
        lax.fori_loop(0, tm // PROJ_ROW_CHUNK, step, 0)
        small_ref[...] = _dot(h_ref[...], ws_ref[...])

    h = h_ref[...]
    for c in range(0, o_ref.shape[1], PROJ_COL_CHUNK):
        cs = slice(c, c + PROJ_COL_CHUNK)
        o_ref[:, cs] = _dot(h, w_ref[:, cs])


def _in_proj(x, nw, w_main, w_small):
    m, d = x.shape
    n = w_main.shape[1]
    tm, tn = TM_PROJ, TN_PROJ
    return pl.pallas_call(
        _in_proj_kernel,
        grid=(m // tm, n // tn),
        in_specs=[
            pl.BlockSpec((tm, d), lambda i, j: (i, 0)),
            pl.BlockSpec((1, d), lambda i, j: (0, 0)),
            pl.BlockSpec((d, tn), lambda i, j: (0, j)),
            pl.BlockSpec((d, SMALL_W), lambda i, j: (0, 0)),
        ],
        out_specs=[
            pl.BlockSpec((tm, tn), lambda i, j: (i, j)),
            pl.BlockSpec((tm, SMALL_W), lambda i, j: (i, 0)),
        ],
        out_shape=[jax.ShapeDtypeStruct((m, n), F32), jax.ShapeDtypeStruct((m, SMALL_W), F32)],
        scratch_shapes=[pltpu.VMEM((tm, d), BF16)],
        compiler_params=_cparams(("parallel", "arbitrary")),
        name="in_proj",
    )(x, nw, w_main, w_small)


def _conv_silu(ext_ref, m_ref, p_ref, n_ref, w_ref, b_ref, has_prev, has_next):
    q = m_ref.shape[0]
    nslab = m_ref.shape[1] // LANES
    for s in range(nslab):
        ls = slice(s * LANES, (s + 1) * LANES)
        ext_ref[s, 0:HALO, :] = jnp.where(has_prev, p_ref[:, ls], 0.0)
        ext_ref[s, HALO:HALO + q, :] = m_ref[:, ls]
        ext_ref[s, HALO + q:HALO + q + HALO, :] = jnp.where(has_next, n_ref[:, ls], 0.0)
    outs = []
    for s in range(nslab):
        ls = slice(s * LANES, (s + 1) * LANES)
        acc = b_ref[:, ls]
        for t in range(CONV_WIDTH):
            r0 = HALO + t - CONV_WIDTH // 2
            acc = acc + ext_ref[s, pl.ds(r0, q, stride=1), :] * w_ref[t:t + 1, ls]
        outs.append(_silu(acc))
    return jnp.concatenate(outs, axis=1)


def _ssd_pre(small_ref, rows, dtb_ref, alog_ref, rev):
    q = SSD_CHUNK
    off = SSD_HEADS if rev else 0
    sm_t = small_ref[rows, :].T
    dt_t = _softplus(sm_t[off:off + SSD_HEADS, :] + dtb_ref[...])
    a_t = dt_t * (-jnp.exp(alog_ref[...]))
    r_i = lax.broadcasted_iota(jnp.int32, (q, q), 0)
    c_i = lax.broadcasted_iota(jnp.int32, (q, q), 1)
    tri = (r_i >= c_i) if rev else (r_i <= c_i)
    tri_bf = jnp.where(tri, 1.0, 0.0).astype(BF16)
    a_hi, a_lo = _split_hi_lo(a_t)
    cum_t = _dot(a_hi, tri_bf) + _dot(a_lo, tri_bf)
    tot_t = cum_t[:, 0:1] if rev else cum_t[:, q - 1:q]
    wend_t = dt_t * jnp.exp(tot_t - cum_t)
    dec_t = jnp.exp(cum_t)
    cum2_t = cum_t * LOG2E
    vt = jnp.concatenate([dt_t, wend_t, dec_t, cum2_t], axis=0)
    vv = vt.T
    lane = lax.broadcasted_iota(jnp.int32, (q, LANES), 1)
    return dict(
        vv=vv, vv_bf=vv.astype(BF16), cum2_t=cum2_t,
        causal=(c_i >= r_i) if rev else (c_i <= r_i),
        lo_half=lane < SSD_HEADDIM)


def _ssd_group(g, ctx, xs, bm, cm, e_ref, s_ref, rev):
    q = SSD_CHUNK
    hpg = SSD_HEADS // SSD_GROUPS
    gw = hpg * SSD_HEADDIM
    gs = slice(g * gw, (g + 1) * gw)
    vv, vv_bf, cum2_t = ctx["vv"], ctx["vv_bf"], ctx["cum2_t"]
    xs_g = xs[:, gs]
    xdt = (xs_g * _dot(vv_bf, e_ref[0, :, gs])).astype(BF16)
    xend = (xs_g * _dot(vv_bf, e_ref[1, :, gs])).astype(BF16)
    dec_x = _dot(vv_bf, e_ref[2, :, gs])
    dec_tot = dec_x[0:1, :] if rev else dec_x[q - 1:q, :]
    cm_g = cm[:, g * SSD_STATE:(g + 1) * SSD_STATE]
    bm_g = bm[:, g * SSD_STATE:(g + 1) * SSD_STATE]
    cb = _dot_nt(cm_g, bm_g)
    y_off = _dot(cm_g, s_ref[g].astype(BF16)) * dec_x
    y_pairs = []
    for p in range(hpg // 2):
        ws = []
        for hh in range(2):
            h = g * hpg + 2 * p + hh
            col = 3 * SSD_HEADS + h
            seg = vv[:, col:col + 1] - cum2_t[h:h + 1, :]
            ws.append((cb * jnp.exp2(jnp.where(ctx["causal"], seg, -jnp.inf))).astype(BF16))
        lhs = jnp.concatenate(ws, axis=1)
        xp = xdt[:, p * LANES:(p + 1) * LANES]
        zero = jnp.zeros_like(xp)
        rhs = jnp.concatenate([jnp.where(ctx["lo_half"], xp, zero), jnp.where(ctx["lo_half"], zero, xp)], axis=0)
        y_pairs.append(_dot(lhs, rhs))
    s_ref[g] = s_ref[g] * dec_tot + _dot_tn(bm_g, xend)
    return jnp.concatenate(y_pairs, axis=1) + y_off


def _gla_pre(q_ref, k_ref, v_ref, small_ref, rows, wg_ref, bg_ref, rev):
    tb, qc = GLA_BLOCK, GLA_CHUNK
    r_i = lax.broadcasted_iota(jnp.int32, (tb, tb), 0)
    c_i = lax.broadcasted_iota(jnp.int32, (tb, tb), 1)
    r_blk = r_i // qc
    c_blk = c_i // qc
    order = (c_i >= r_i) if rev else (c_i <= r_i)
    diag_f = jnp.where(r_blk == c_blk, jnp.where(order, 1.0, 0.0), 0.0)
    tri_bf = diag_f.astype(BF16)

    low = small_ref[rows, :].astype(BF16)
    pre = _dot(low, wg_ref[...]) + bg_ref[...]
    gk2 = _softplus(-pre) * (-LOG2E / GLA_NORMALIZER)
    g_hi, g_lo = _split_hi_lo(gk2)
    gg = _dot(tri_bf, g_hi) + _dot(tri_bf, g_lo)

    def rows2(a0, a1):
        n = a0.shape[1]
        return jnp.concatenate([jnp.broadcast_to(a0, (qc, n)), jnp.broadcast_to(a1, (qc, n))], axis=0)

    mid = qc // 2 if rev else qc // 2 - 1
    last = 0 if rev else qc - 1
    gmid = rows2(gg[mid:mid + 1, :], gg[qc + mid:qc + mid + 1, :])
    gl0 = gg[last:last + 1, :]
    gl1 = gg[qc + last:qc + last + 1, :]
    glast = rows2(gl0, gl1)
    dec0 = jnp.exp2(gl0)
    dec1 = jnp.exp2(gl1)
    dec_tot = jnp.exp2(gl0 + gl1)

    qe_f = q_ref[rows, :] * jnp.exp2(gg - (gmid + 0.5 * jnp.log2(float(GLA_HEAD_K))))
    ke_f = k_ref[rows, :] * jnp.exp2(gmid - gg)
    qe = qe_f.astype(BF16)
    ke = ke_f.astype(BF16)
    qin = qe_f * jnp.exp2(gmid)
    kend = ke_f * jnp.exp2(glast - gmid)
    if rev:
        qin_x = jnp.concatenate([qin[:qc] * dec1, qin[qc:]], axis=0)
        kend_x = jnp.concatenate([kend[:qc], kend[qc:] * dec0], axis=0)
    else:
        qin_x = jnp.concatenate([qin[:qc], qin[qc:] * dec0], axis=0)
        kend_x = jnp.concatenate([kend[:qc] * dec1, kend[qc:]], axis=0)
    return dict(
        qe=qe, ke=ke, qin=qin.astype(BF16), kend=kend.astype(BF16),
        qin_x=qin_x.astype(BF16), kend_x=kend_x.astype(BF16),
        vb=v_ref[rows, :].astype(BF16),
        decb=jnp.broadcast_to(dec_tot, (LANES, D_GLA_K)),
        diag=diag_f > 0.5,
        offd=(c_blk - r_blk == 1) if rev else (r_blk - c_blk == 1))


def _gla_head(h, ctx, s_ref):
    ks = slice(h * GLA_HEAD_K, (h + 1) * GLA_HEAD_K)
    vs = slice(h * GLA_HEAD_V, (h + 1) * GLA_HEAD_V)
    a_diag = _dot_nt(ctx["qe"][:, ks], ctx["ke"][:, ks])
    a_off = _dot_nt(ctx["qin"][:, ks], ctx["kend"][:, ks])
    a = jnp.where(ctx["diag"], a_diag, jnp.where(ctx["offd"], a_off, 0.0)).astype(BF16)
    s = s_ref[h]
    vb_h = ctx["vb"][:, vs]
    o_h = _dot(a, vb_h) + _dot(ctx["qin_x"][:, ks], s.astype(BF16))
    upd = _dot_tn(ctx["kend_x"][:, ks], vb_h)
    decb = ctx["decb"]
    dcol = jnp.concatenate(
        [decb[:, h * GLA_HEAD_K + t * LANES:h * GLA_HEAD_K + (t + 1) * LANES].T for t in range(GLA_HEAD_K // LANES)],
        axis=0)
    s_ref[h] = s * jnp.concatenate([dcol] * (GLA_HEAD_V // LANES), axis=1) + upd
    return o_h


def _chunk_position(rev, cps):
    step = pl.program_id(0)
    c = (pl.num_programs(0) - 1 - step) if rev else step
    pos = c % cps
    first = pos == ((cps - 1) if rev else 0)
    return pos, first


def _mix_fwd_kernel(xs_m, xs_p, xs_n, bc_m, bc_p, bc_n, small_ref, q_ref, k_ref, v_ref,
                    cwx_ref, cbx_ref, cwbc_ref, cbbc_ref, dtb_ref, alog_ref, e_ref, wg_ref, bg_ref,
                    yf_ref, xsc_ref, bcc_ref, of_ref,
                    s_ssd, s_gla, ext_x, ext_bc, *, cps):
    pos, first = _chunk_position(False, cps)

    @pl.when(first)
    def _():
        s_ssd[...] = jnp.zeros_like(s_ssd)
        s_gla[...] = jnp.zeros_like(s_gla)

    has_prev = pos != 0
    has_next = pos != cps - 1
    xs = _conv_silu(ext_x, xs_m, xs_p, xs_n, cwx_ref, cbx_ref, has_prev, has_next)
    bc = _conv_silu(ext_bc, bc_m, bc_p, bc_n, cwbc_ref, cbbc_ref, has_prev, has_next).astype(BF16)
    xsc_ref[...] = xs
    bcc_ref[...] = bc
    ng = SSD_GROUPS * SSD_STATE
    gw = D_SSD // SSD_GROUPS
    for sc in range(MIX_CHUNKS_FWD):
        rows = slice(sc * SSD_CHUNK, (sc + 1) * SSD_CHUNK)
        sctx = _ssd_pre(small_ref, rows, dtb_ref, alog_ref, False)
        gctx = _gla_pre(q_ref, k_ref, v_ref, small_ref, rows, wg_ref, bg_ref, False)
        for i in range(SSD_GROUPS):
            of_ref[rows, i * GLA_HEAD_V:(i + 1) * GLA_HEAD_V] = _gla_head(i, gctx, s_gla)
            yf_ref[rows, i * gw:(i + 1) * gw] = _ssd_group(
                i, sctx, xs[rows, :], bc[rows, :ng], bc[rows, ng:], e_ref, s_ssd, False)


def _mix_bwd_kernel(xsc_ref, bcc_ref, small_ref, z_ref, yf_ref, q_ref, k_ref, v_ref, gout_ref, of_ref,
                    dtb_ref, alog_ref, e_ref, dskip_ref, snw_ref, wg_ref, bg_ref, gnw_ref,
                    y_ref, o_ref,
                    s_ssd, s_gla, *, cps):
    _, first = _chunk_position(True, cps)

    @pl.when(first)
    def _():
        s_ssd[...] = jnp.zeros_like(s_ssd)
        s_gla[...] = jnp.zeros_like(s_gla)

    ng = SSD_GROUPS * SSD_STATE
    gw = D_SSD // SSD_GROUPS
    for sc in reversed(range(MIX_CHUNKS_BWD)):
        rows = slice(sc * SSD_CHUNK, (sc + 1) * SSD_CHUNK)
        sctx = _ssd_pre(small_ref, rows, dtb_ref, alog_ref, True)
        gctx = _gla_pre(q_ref, k_ref, v_ref, small_ref, rows, wg_ref, bg_ref, True)
        xs = xsc_ref[rows, :]
        bc = bcc_ref[rows, :]
        for i in range(SSD_GROUPS):
            vs = slice(i * GLA_HEAD_V, (i + 1) * GLA_HEAD_V)
            o_h = _rms_normalize(_gla_head(i, gctx, s_gla) + of_ref[rows, vs], gnw_ref[...])
            gz = gout_ref[rows, vs]
            o_ref[rows, vs] = (o_h * _silu(gz)).astype(o_ref.dtype)

            gs = slice(i * gw, (i + 1) * gw)
            y = _ssd_group(i, sctx, xs, bc[:, :ng], bc[:, ng:], e_ref, s_ssd, True)
            y = y + yf_ref[rows, gs] + dskip_ref[:, gs] * xs[:, gs]
            zz = z_ref[rows, gs]
            y = y * _silu(zz)
            y_ref[rows, gs] = _rms_normalize(y, snw_ref[:, gs]).astype(y_ref.dtype)


def _const_spec(a):
    nd = a.ndim
    return pl.BlockSpec(a.shape, lambda s: (0,) * nd)


_MIX_SCRATCH = [
    pltpu.VMEM((SSD_GROUPS, SSD_STATE, D_SSD // SSD_GROUPS), F32),
    pltpu.VMEM((GLA_HEADS, GLA_HEAD_K, GLA_HEAD_V), F32),
]


def _mix_fwd(proj, small, prm, cps):
    m = proj.shape[0]
    q = MIX_CHUNKS_FWD * SSD_CHUNK
    nc = m // q
    rb = q // HALO
    last_hb = m // HALO - 1

    def main(colblk):
        return lambda s: (s, colblk)

    def prev(colblk):
        return lambda s: (jnp.maximum(s * rb - 1, 0), colblk)

    def nxt(colblk):
        return lambda s: (jnp.minimum((s + 1) * rb, last_hb), colblk)

    xs_blk, bc_blk = COL_XS // 2048, COL_BC // 1024
    in_specs = [
        pl.BlockSpec((q, 2048), main(xs_blk)),
        pl.BlockSpec((HALO, 2048), prev(xs_blk)),
        pl.BlockSpec((HALO, 2048), nxt(xs_blk)),
        pl.BlockSpec((q, 1024), main(bc_blk)),
        pl.BlockSpec((HALO, 1024), prev(bc_blk)),
        pl.BlockSpec((HALO, 1024), nxt(bc_blk)),
        pl.BlockSpec((q, SMALL_W), main(0)),
        pl.BlockSpec((q, D_GLA_K), main(COL_Q // D_GLA_K)),
        pl.BlockSpec((q, D_GLA_K), main(COL_K // D_GLA_K)),
        pl.BlockSpec((q, D_GLA_V), main(COL_V // D_GLA_V)),
    ]
    params = [prm["cw_x"], prm["cb_x"], prm["cw_bc"], prm["cb_bc"], prm["dtb_f"], prm["alog_f"], prm["expand"],
              prm["gate_w_f"], prm["gate_b_f"]]
    in_specs += [_const_spec(a) for a in params]
    return pl.pallas_call(
        functools.partial(_mix_fwd_kernel, cps=cps),
        grid=(nc,),
        in_specs=in_specs,
        out_specs=[
            pl.BlockSpec((q, D_SSD), main(0)),
            pl.BlockSpec((q, D_SSD), main(0)),
            pl.BlockSpec((q, 2 * SSD_GROUPS * SSD_STATE), main(0)),
            pl.BlockSpec((q, D_GLA_V), main(0)),
        ],
        out_shape=[
            jax.ShapeDtypeStruct((m, D_SSD), F32),
            jax.ShapeDtypeStruct((m, D_SSD), F32),
            jax.ShapeDtypeStruct((m, 2 * SSD_GROUPS * SSD_STATE), BF16),
            jax.ShapeDtypeStruct((m, D_GLA_V), F32),
        ],
        scratch_shapes=_MIX_SCRATCH + [
            pltpu.VMEM((D_SSD // LANES, q + 2 * HALO, LANES), F32),
            pltpu.VMEM((2 * SSD_GROUPS * SSD_STATE // LANES, q + 2 * HALO, LANES), F32),
        ],
        compiler_params=_cparams(("arbitrary",)),
        name="mix_fwd",
    )(*([proj] * 6), small, *([proj] * 3), *params)


def _mix_bwd(proj, small, y_f, xs_c, bc_c, o_f, prm, cps):
    m = proj.shape[0]
    q = MIX_CHUNKS_BWD * SSD_CHUNK
    nc = m // q

    def main(colblk):
        return lambda s: (nc - 1 - s, colblk)

    in_specs = [
        pl.BlockSpec((q, D_SSD), main(0)),
        pl.BlockSpec((q, 2 * SSD_GROUPS * SSD_STATE), main(0)),
        pl.BlockSpec((q, SMALL_W), main(0)),
        pl.BlockSpec((q, 2048), main(COL_Z // 2048)),
        pl.BlockSpec((q, D_SSD), main(0)),
        pl.BlockSpec((q, D_GLA_K), main(COL_Q // D_GLA_K)),
        pl.BlockSpec((q, D_GLA_K), main(COL_K // D_GLA_K)),
        pl.BlockSpec((q, D_GLA_V), main(COL_V // D_GLA_V)),
        pl.BlockSpec((q, D_GLA_V), main(COL_GOUT // D_GLA_V)),
        pl.BlockSpec((q, D_GLA_V), main(0)),
    ]
    params = [prm["dtb_b"], prm["alog_b"], prm["expand"], prm["dskip_x"], prm["ssd_nw"],
              prm["gate_w_b"], prm["gate_b_b"], prm["gla_nw"]]
    in_specs += [_const_spec(a) for a in params]
    return pl.pallas_call(
        functools.partial(_mix_bwd_kernel, cps=cps),
        grid=(nc,),
        in_specs=in_specs,
        out_specs=[pl.BlockSpec((q, D_SSD), main(0)), pl.BlockSpec((q, D_GLA_V), main(0))],
        out_shape=[jax.ShapeDtypeStruct((m, D_SSD), BF16), jax.ShapeDtypeStruct((m, D_GLA_V), BF16)],
        scratch_shapes=_MIX_SCRATCH,
        compiler_params=_cparams(("arbitrary",)),
        name="mix_bwd",
    )(xs_c, bc_c, small, proj, y_f, proj, proj, proj, proj, o_f, *params)


def _out_proj_kernel(x_ref, ya_ref, yb_ref, wa_ref, wb_ref, o_ref):
    o_ref[...] = x_ref[...] + _dot(ya_ref[...], wa_ref[...]) + _dot(yb_ref[...], wb_ref[...])


def _out_proj(x, ya, yb, wa, wb):
    m, d = x.shape
    ka = ya.shape[1]
    kb = yb.shape[1]
    tm, tn = TM_OUT, TN_OUT
    return pl.pallas_call(
        _out_proj_kernel,
        grid=(m // tm, d // tn),
        in_specs=[
            pl.BlockSpec((tm, tn), lambda i, j: (i, j)),
            pl.BlockSpec((tm, ka), lambda i, j: (i, 0)),
            pl.BlockSpec((tm, kb), lambda i, j: (i, 0)),
            pl.BlockSpec((ka, tn), lambda i, j: (0, j)),
            pl.BlockSpec((kb, tn), lambda i, j: (0, j)),
        ],
        out_specs=pl.BlockSpec((tm, tn), lambda i, j: (i, j)),
        out_shape=jax.ShapeDtypeStruct((m, d), F32),
        compiler_params=_cparams(("parallel", "arbitrary")),
        name="out_proj",
    )(x, ya, yb, wa, wb)


def _xattn_kernel(x_ref, nw_ref, wq_ref, kv_ref, wo_ref, o_ref):
    x = x_ref[...]
    h = _rms_normalize(x, nw_ref[...]).astype(BF16)
    qq = _dot(h, wq_ref[...]).astype(BF16)
    scale = XATTN_HEAD_DIM ** -0.5
    heads = []
    for hd in range(XATTN_HEADS):
        ds = slice(hd * XATTN_HEAD_DIM, (hd + 1) * XATTN_HEAD_DIM)
        kh = kv_ref[:, ds]
        vh = kv_ref[:, D_MODEL + hd * XATTN_HEAD_DIM:D_MODEL + (hd + 1) * XATTN_HEAD_DIM]
        s = _dot_nt(qq[:, ds], kh) * scale
        s = s - jnp.max(s, axis=-1, keepdims=True)
        e = jnp.exp(s)
        p = e / jnp.sum(e, axis=-1, keepdims=True)
        heads.append(_dot(p.astype(BF16), vh).astype(BF16))
    o = jnp.concatenate(heads, axis=1)
    o_ref[...] = x + _dot(o, wo_ref[...])


def _xattn(x, nw, wq, kv, wo, tiles_per_seq):
    m, d = x.shape
    tm = TM_XATTN
    single = pl.Buffered(1)
    return pl.pallas_call(
        _xattn_kernel,
        grid=(m // tm,),
        in_specs=[
            pl.BlockSpec((tm, d), lambda i: (i, 0)),
            pl.BlockSpec((1, d), lambda i: (0, 0)),
            pl.BlockSpec((d, d), lambda i: (0, 0), pipeline_mode=single),
            pl.BlockSpec((N_MEM, 2 * d), lambda i: (i // tiles_per_seq, 0)),
            pl.BlockSpec((d, d), lambda i: (0, 0), pipeline_mode=single),
        ],
        out_specs=pl.BlockSpec((tm, d), lambda i: (i, 0)),
        out_shape=jax.ShapeDtypeStruct((m, d), F32),
        compiler_params=_cparams(("arbitrary",)),
        name="xattn",
    )(x, nw, wq, kv, wo)


_W_IN_SEGMENTS = (
    (COL_Z, 0, 4096),
    (COL_V, 7232, 2048),
    (COL_GOUT, 9312, 2048),
    (COL_BC, 4096, 1024),
    (COL_Q, 5184, 1024),
    (COL_K, 6208, 1024),
)
_W_IN_DT = (5120, 5184)
_W_IN_LOW = (9280, 9312)
W_RELAYOUT_ROWS = 256


def _w_in_relayout_kernel(w_ref, main_ref, small_ref):
    for dst, src, width in _W_IN_SEGMENTS:
        main_ref[:, dst:dst + width] = w_ref[:, src:src + width].astype(BF16)
    rows = w_ref.shape[0]
    narrow = _W_IN_DT[1] - _W_IN_DT[0] + _W_IN_LOW[1] - _W_IN_LOW[0]
    small_ref[...] = jnp.concatenate(
        [w_ref[:, _W_IN_DT[0]:_W_IN_DT[1]], w_ref[:, _W_IN_LOW[0]:_W_IN_LOW[1]],
         jnp.zeros((rows, SMALL_W - narrow), F32)], axis=1).astype(BF16)


def _relayout_w_in(w_in):
    d, n_in = w_in.shape
    tr = W_RELAYOUT_ROWS
    return pl.pallas_call(
        _w_in_relayout_kernel,
        grid=(d // tr,),
        in_specs=[pl.BlockSpec((tr, n_in), lambda i: (i, 0))],
        out_specs=[pl.BlockSpec((tr, N_PROJ), lambda i: (i, 0)), pl.BlockSpec((tr, SMALL_W), lambda i: (i, 0))],
        out_shape=[jax.ShapeDtypeStruct((d, N_PROJ), BF16), jax.ShapeDtypeStruct((d, SMALL_W), BF16)],
        compiler_params=_cparams(("arbitrary",)),
        name="w_in_relayout",
    )(w_in)


def _prepare(p):
    w_main, w_small = _relayout_w_in(p["w_in"])

    def gate_w(w, row0):
        full = jnp.zeros((SMALL_W, D_GLA_K), F32)
        return full.at[row0:row0 + GLA_RANK].set(w).astype(BF16)

    head_of_lane = jnp.arange(D_SSD) // SSD_HEADDIM
    rows = jnp.arange(4 * SSD_HEADS)[:, None]
    expand = jnp.stack([(rows == (head_of_lane[None, :] + SSD_HEADS * t)) for t in range(3)], axis=0)
    conv_w, conv_b = p["conv_w"], p["conv_b"]
    return dict(
        w_main=w_main, w_small=w_small,
        cw_x=conv_w[:, :D_SSD], cb_x=conv_b[None, :D_SSD],
        cw_bc=conv_w[:, D_SSD:], cb_bc=conv_b[None, D_SSD:],
        dtb_f=p["dt_bias_fwd"][:, None], dtb_b=p["dt_bias_bwd"][:, None],
        alog_f=p["a_log_fwd"][:, None], alog_b=p["a_log_bwd"][:, None],
        expand=expand.astype(BF16),
        dskip_x=jnp.repeat(p["d_skip"], SSD_HEADDIM)[None, :],
        ssd_nw=p["ssd_norm"][None, :],
        gate_w_f=gate_w(p["gla_gate_w_fwd"], 2 * SSD_HEADS),
        gate_w_b=gate_w(p["gla_gate_w_bwd"], 2 * SSD_HEADS + GLA_RANK),
        gate_b_f=p["gla_gate_b_fwd"][None, :], gate_b_b=p["gla_gate_b_bwd"][None, :],
        gla_nw=p["gla_norm"][None, :],
        w_out_a=p["w_out"][:D_SSD].astype(BF16), w_out_b=p["w_out"][D_SSD:].astype(BF16),
        ffn1=(p["ffn1_norm"][None, :], p["ffn1_w1"].astype(BF16), p["ffn1_w3"].astype(BF16), p["ffn1_w2"].astype(BF16)),
        ffn2=(p["ffn2_norm"][None, :], p["ffn2_w1"].astype(BF16), p["ffn2_w3"].astype(BF16), p["ffn2_w2"].astype(BF16)),
        mix_nw=p["mix_norm"][None, :],
        xattn_nw=p["xattn_norm"][None, :], mem_nw=p["mem_norm"][None, :],
        w_cq=p["w_cq"].astype(BF16), w_ckv=p["w_ckv"].astype(BF16), w_co=p["w_co"].astype(BF16),
        final_nw=p["final_norm"][None, :],
    )


def _trunk(x3, mem3, prm):
    b, l, d = x3.shape
    x = x3.reshape(b * l, d)
    mem = mem3.reshape(b * N_MEM, d)
    x = _ffn(x, *prm["ffn1"])
    proj, small = _in_proj(x, prm["mix_nw"], prm["w_main"], prm["w_small"])
    y_f, xs_c, bc_c, o_f = _mix_fwd(proj, small, prm, l // (MIX_CHUNKS_FWD * SSD_CHUNK))
    y, o = _mix_bwd(proj, small, y_f, xs_c, bc_c, o_f, prm, l // (MIX_CHUNKS_BWD * SSD_CHUNK))
    x = _out_proj(x, y, o, prm["w_out_a"], prm["w_out_b"])
    kv = _norm_matmul(mem, prm["mem_nw"], prm["w_ckv"], BF16, N_MEM, 1024, "kv_proj")
    x = _xattn(x, prm["xattn_nw"], prm["w_cq"], kv, prm["w_co"], l // TM_XATTN)
    x = _ffn(x, *prm["ffn2"], final_nw=prm["final_nw"])
    return x.reshape(b, l, d)


def kernel(x_prompt, x_sample, mem_prompt, mem_sample, ffn1_norm, ffn1_w1, ffn1_w3, ffn1_w2, mix_norm, w_in, conv_w, conv_b, dt_bias_fwd, dt_bias_bwd, a_log_fwd, a_log_bwd, d_skip, ssd_norm, gla_gate_w_fwd, gla_gate_b_fwd, gla_gate_w_bwd, gla_gate_b_bwd, gla_norm, w_out, xattn_norm, mem_norm, w_cq, w_ckv, w_co, ffn2_norm, ffn2_w1, ffn2_w3, ffn2_w2, final_norm):
    p = dict(
        ffn1_norm=ffn1_norm[0], ffn1_w1=ffn1_w1[0], ffn1_w3=ffn1_w3[0], ffn1_w2=ffn1_w2[0],
        mix_norm=mix_norm[0], w_in=w_in[0], conv_w=conv_w[0], conv_b=conv_b[0],
        dt_bias_fwd=dt_bias_fwd[0], dt_bias_bwd=dt_bias_bwd[0], a_log_fwd=a_log_fwd[0], a_log_bwd=a_log_bwd[0],
        d_skip=d_skip[0], ssd_norm=ssd_norm[0],
        gla_gate_w_fwd=gla_gate_w_fwd[0], gla_gate_b_fwd=gla_gate_b_fwd[0],
        gla_gate_w_bwd=gla_gate_w_bwd[0], gla_gate_b_bwd=gla_gate_b_bwd[0],
        gla_norm=gla_norm[0], w_out=w_out[0], xattn_norm=xattn_norm[0], mem_norm=mem_norm[0],
        w_cq=w_cq[0], w_ckv=w_ckv[0], w_co=w_co[0],
        ffn2_norm=ffn2_norm[0], ffn2_w1=ffn2_w1[0], ffn2_w3=ffn2_w3[0], ffn2_w2=ffn2_w2[0],
        final_norm=final_norm,
    )
    prm = _prepare(p)
    return (_trunk(x_prompt, mem_prompt, prm), _trunk(x_sample, mem_sample, prm))
```

```python
import functools

import jax
import jax.numpy as jnp
from jax import lax
from jax.experimental import pallas as pl
from jax.experimental.pallas import tpu as pltpu

F32 = jnp.float32
BF16 = jnp.bfloat16

D_MODEL = 2048
N_MEM = 256
D_SSD = 2048
SSD_HEADS = 32
SSD_HEADDIM = 64
SSD_GROUPS = 4
SSD_STATE = 128
SSD_CHUNK = 128
CONV_WIDTH = 5
GLA_HEADS = 4
GLA_HEAD_K = 256
GLA_HEAD_V = 512
D_GLA_K = 1024
D_GLA_V = 2048
GLA_RANK = 16
GLA_NORMALIZER = 16.0
GLA_CHUNK = 64
XATTN_HEADS = 4
XATTN_HEAD_DIM = 512
D_FF = 5632
EPS = 1e-6

COL_Z = 0
COL_XS = 2048
COL_V = 4096
COL_GOUT = 6144
COL_BC = 8192
COL_Q = 9216
COL_K = 10240
N_PROJ = 11264
SMALL_W = 128

SUBLANE = 8
LANES = 128
HALO = SUBLANE
TM_FFN = 1024
TF_FFN = 512
FFN_ROW_CHUNK = 256
FFN_COL_CHUNK = 512
TM_PROJ = 1024
TN_PROJ = 1024
PROJ_ROW_CHUNK = 256
PROJ_COL_CHUNK = 512
TM_OUT = 1024
TN_OUT = 512
TM_XATTN = 512
GLA_BLOCK = SSD_CHUNK
MIX_CHUNKS_FWD = 2
MIX_CHUNKS_BWD = 1
LOG2E = 1.4426950408889634
VMEM_LIMIT = 56 * 1024 * 1024
VMEM_LIMIT_FFN = 62 * 1024 * 1024


def _cparams(sem, vmem_limit=VMEM_LIMIT):
    return pltpu.CompilerParams(dimension_semantics=sem, vmem_limit_bytes=vmem_limit)


def _rms_normalize(x, w):
    ms = jnp.mean(x * x, axis=-1, keepdims=True)
    return x * lax.rsqrt(ms + EPS) * w


def _silu(x):
    h = 0.5 * x
    return h + h * jnp.tanh(h)


def _softplus(x):
    return jnp.maximum(x, 0.0) + jnp.log(1.0 + jnp.exp(-jnp.abs(x)))


def _split_hi_lo(x):
    hi = x.astype(BF16)
    lo = (x - hi.astype(F32)).astype(BF16)
    return hi, lo


def _dot(a, b):
    return jnp.dot(a, b, preferred_element_type=F32)


def _dot_nt(a, b):
    return lax.dot_general(a, b, (((1,), (1,)), ((), ())), preferred_element_type=F32)


def _dot_tn(a, b):
    return lax.dot_general(a, b, (((0,), (0,)), ((), ())), preferred_element_type=F32)


def _ffn_kernel(x_ref, nw_ref, w1_ref, w3_ref, w2_ref, *rest, final):
    if final:
        fnw_ref, o_ref, h_ref = rest
    else:
        o_ref, h_ref = rest
    j = pl.program_id(1)

    tm, d = o_ref.shape

    def for_row_chunks(body):
        def step(r, carry):
            body(pl.ds(pl.multiple_of(r * FFN_ROW_CHUNK, FFN_ROW_CHUNK), FFN_ROW_CHUNK))
            return carry
        lax.fori_loop(0, tm // FFN_ROW_CHUNK, step, 0)

    @pl.when(j == 0)
    def _():
        def body(rs):
            h_ref[rs, :] = _rms_normalize(x_ref[rs, :], nw_ref[...]).astype(BF16)
            o_ref[rs, :] = jnp.zeros((FFN_ROW_CHUNK, d), F32)
        for_row_chunks(body)

    h = h_ref[...]
    g = _dot(h, w1_ref[...])
    u = _dot(h, w3_ref[...])
    a = (g * jax.nn.sigmoid(g) * u).astype(BF16)
    for c in range(0, d, FFN_COL_CHUNK):
        cs = slice(c, c + FFN_COL_CHUNK)
        o_ref[:, cs] += _dot(a, w2_ref[:, cs])

    @pl.when(j == pl.num_programs(1) - 1)
    def _():
        def body(rs):
            y = x_ref[rs, :] + 0.5 * o_ref[rs, :]
            if final:
                y = _rms_normalize(y, fnw_ref[...])
            o_ref[rs, :] = y
        for_row_chunks(body)


def _ffn(x, nw, w1, w3, w2, final_nw=None):
    m, d = x.shape
    dff = w1.shape[1]
    tm, tf = TM_FFN, TF_FFN
    final = final_nw is not None
    in_specs = [
        pl.BlockSpec((tm, d), lambda i, j: (i, 0)),
        pl.BlockSpec((1, d), lambda i, j: (0, 0)),
        pl.BlockSpec((d, tf), lambda i, j: (0, j)),
        pl.BlockSpec((d, tf), lambda i, j: (0, j)),
        pl.BlockSpec((tf, d), lambda i, j: (j, 0)),
    ]
    args = [x, nw, w1, w3, w2]
    if final:
        in_specs.append(pl.BlockSpec((1, d), lambda i, j: (0, 0)))
        args.append(final_nw)
    return pl.pallas_call(
        functools.partial(_ffn_kernel, final=final),
        grid=(m // tm, dff // tf),
        in_specs=in_specs,
        out_specs=pl.BlockSpec((tm, d), lambda i, j: (i, 0)),
        out_shape=jax.ShapeDtypeStruct((m, d), F32),
        scratch_shapes=[pltpu.VMEM((tm, d), BF16)],
        compiler_params=_cparams(("parallel", "arbitrary"), VMEM_LIMIT_FFN),
        name="ffn_final" if final else "ffn",
    )(*args)


def _norm_matmul_kernel(x_ref, nw_ref, w_ref, o_ref, h_ref):
    @pl.when(pl.program_id(1) == 0)
    def _():
        h_ref[...] = _rms_normalize(x_ref[...], nw_ref[...]).astype(BF16)

    o_ref[...] = _dot(h_ref[...], w_ref[...]).astype(o_ref.dtype)


def _norm_matmul(x, nw, w, out_dtype, tm, tn, name):
    m, d = x.shape
    n = w.shape[1]
    return pl.pallas_call(
        _norm_matmul_kernel,
        grid=(m // tm, n // tn),
        in_specs=[
            pl.BlockSpec((tm, d), lambda i, j: (i, 0)),
            pl.BlockSpec((1, d), lambda i, j: (0, 0)),
            pl.BlockSpec((d, tn), lambda i, j: (0, j)),
        ],
        out_specs=pl.BlockSpec((tm, tn), lambda i, j: (i, j)),
        out_shape=jax.ShapeDtypeStruct((m, n), out_dtype),
        scratch_shapes=[pltpu.VMEM((tm, d), BF16)],
        compiler_params=_cparams(("parallel", "arbitrary")),
        name=name,
    )(x, nw, w)


def _in_proj_kernel(x_ref, nw_ref, w_ref, ws_ref, o_ref, small_ref, h_ref):
    tm = h_ref.shape[0]

    @pl.when(pl.program_id(1) == 0)
    def _():
        def step(r, carry):
            rs = pl.ds(pl.multiple_of(r * PROJ_ROW_CHUNK, PROJ_ROW_CHUNK), PROJ_ROW_CHUNK)
            h_ref[rs, :] = _rms_normalize(x_ref[rs, :], nw_ref[...]).astype(BF16)
            return carry
        lax.fori_loop(0, tm // PROJ_ROW_CHUNK, step, 0)
        small_ref[...] = _dot(h_ref[...], ws_ref[...])

    h = h_ref[...]
    for c in range(0, o_ref.shape[1], PROJ_COL_CHUNK):
        cs = slice(c, c + PROJ_COL_CHUNK)
        o_ref[:, cs] = _dot(h, w_ref[:, cs])


def _in_proj(x, nw, w_main, w_small):
    m, d = x.shape
    n = w_main.shape[1]
    tm, tn = TM_PROJ, TN_PROJ
    return pl.pallas_call(
        _in_proj_kernel,
        grid=(m // tm, n // tn),
        in_specs=[
            pl.BlockSpec((tm, d), lambda i, j: (i, 0)),
            pl.BlockSpec((1, d), lambda i, j: (0, 0)),
            pl.BlockSpec((d, tn), lambda i, j: (0, j)),
            pl.BlockSpec((d, SMALL_W), lambda i, j: (0, 0)),
        ],
        out_specs=[
            pl.BlockSpec((tm, tn), lambda i, j: (i, j)),
            pl.BlockSpec((tm, SMALL_W), lambda i, j: (i, 0)),
        ],
        out_shape=[jax.ShapeDtypeStruct((m, n), F32), jax.ShapeDtypeStruct((m, SMALL_W), F32)],
        scratch_shapes=[pltpu.VMEM((tm, d), BF16)],
        compiler_params=_cparams(("parallel", "arbitrary")),
        name="in_proj",
    )(x, nw, w_main, w_small)


def _conv_silu(ext_ref, m_ref, p_ref, n_ref, w_ref, b_ref, has_prev, has_next):
    q = m_ref.shape[0]
    nslab = m_ref.shape[1] // LANES
    for s in range(nslab):
        ls = slice(s * LANES, (s + 1) * LANES)
        ext_ref[s, 0:HALO, :] = jnp.where(has_prev, p_ref[:, ls], 0.0)
        ext_ref[s, HALO:HALO + q, :] = m_ref[:, ls]
        ext_ref[s, HALO + q:HALO + q + HALO, :] = jnp.where(has_next, n_ref[:, ls], 0.0)
    outs = []
    for s in range(nslab):
        ls = slice(s * LANES, (s + 1) * LANES)
        acc = b_ref[:, ls]
        for t in range(CONV_WIDTH):
            r0 = HALO + t - CONV_WIDTH // 2
            acc = acc + ext_ref[s, pl.ds(r0, q, stride=1), :] * w_ref[t:t + 1, ls]
        outs.append(_silu(acc))
    return jnp.concatenate(outs, axis=1)


def _ssd_pre(small_ref, rows, dtb_ref, alog_ref, rev):
    q = SSD_CHUNK
    off = SSD_HEADS if rev else 0
    sm_t = small_ref[rows, :].T
    dt_t = _softplus(sm_t[off:off + SSD_HEADS, :] + dtb_ref[...])
    a_t = dt_t * (-jnp.exp(alog_ref[...]))
    r_i = lax.broadcasted_iota(jnp.int32, (q, q), 0)
    c_i = lax.broadcasted_iota(jnp.int32, (q, q), 1)
    tri = (r_i >= c_i) if rev else (r_i <= c_i)
    tri_bf = jnp.where(tri, 1.0, 0.0).astype(BF16)
    a_hi, a_lo = _split_hi_lo(a_t)
    cum_t = _dot(a_hi, tri_bf) + _dot(a_lo, tri_bf)
    tot_t = cum_t[:, 0:1] if rev else cum_t[:, q - 1:q]
    wend_t = dt_t * jnp.exp(tot_t - cum_t)
    dec_t = jnp.exp(cum_t)
    cum2_t = cum_t * LOG2E
    vt = jnp.concatenate([dt_t, wend_t, dec_t, cum2_t], axis=0)
    vv = vt.T
    lane = lax.broadcasted_iota(jnp.int32, (q, LANES), 1)
    return dict(
        vv=vv, vv_bf=vv.astype(BF16), cum2_t=cum2_t,
        causal=(c_i >= r_i) if rev else (c_i <= r_i),
        lo_half=lane < SSD_HEADDIM)


def _ssd_group(g, ctx, xs, bm, cm, e_ref, s_ref, rev):
    q = SSD_CHUNK
    hpg = SSD_HEADS // SSD_GROUPS
    gw = hpg * SSD_HEADDIM
    gs = slice(g * gw, (g + 1) * gw)
    vv, vv_bf, cum2_t = ctx["vv"], ctx["vv_bf"], ctx["cum2_t"]
    xs_g = xs[:, gs]
    xdt = (xs_g * _dot(vv_bf, e_ref[0, :, gs])).astype(BF16)
    xend = (xs_g * _dot(vv_bf, e_ref[1, :, gs])).astype(BF16)
    dec_x = _dot(vv_bf, e_ref[2, :, gs])
    dec_tot = dec_x[0:1, :] if rev else dec_x[q - 1:q, :]
    cm_g = cm[:, g * SSD_STATE:(g + 1) * SSD_STATE]
    bm_g = bm[:, g * SSD_STATE:(g + 1) * SSD_STATE]
    cb = _dot_nt(cm_g, bm_g)
    y_off = _dot(cm_g, s_ref[g].astype(BF16)) * dec_x
    y_pairs = []
    for p in range(hpg // 2):
        ws = []
        for hh in range(2):
            h = g * hpg + 2 * p + hh
            col = 3 * SSD_HEADS + h
            seg = vv[:, col:col + 1] - cum2_t[h:h + 1, :]
            ws.append((cb * jnp.exp2(jnp.where(ctx["causal"], seg, -jnp.inf))).astype(BF16))
        lhs = jnp.concatenate(ws, axis=1)
        xp = xdt[:, p * LANES:(p + 1) * LANES]
        zero = jnp.zeros_like(xp)
        rhs = jnp.concatenate([jnp.where(ctx["lo_half"], xp, zero), jnp.where(ctx["lo_half"], zero, xp)], axis=0)
        y_pairs.append(_dot(lhs, rhs))
    s_ref[g] = s_ref[g] * dec_tot + _dot_tn(bm_g, xend)
    return jnp.concatenate(y_pairs, axis=1) + y_off


def _gla_pre(q_ref, k_ref, v_ref, small_ref, rows, wg_ref, bg_ref, rev):
    tb, qc = GLA_BLOCK, GLA_CHUNK
    r_i = lax.broadcasted_iota(jnp.int32, (tb, tb), 0)
    c_i = lax.broadcasted_iota(jnp.int32, (tb, tb), 1)
    r_blk = r_i // qc
    c_blk = c_i // qc
    order = (c_i >= r_i) if rev else (c_i <= r_i)
    diag_f = jnp.where(r_blk == c_blk, jnp.where(order, 1.0, 0.0), 0.0)
    tri_bf = diag_f.astype(BF16)

    low = small_ref[rows, :].astype(BF16)
    pre = _dot(low, wg_ref[...]) + bg_ref[...]
    gk2 = _softplus(-pre) * (-LOG2E / GLA_NORMALIZER)
    g_hi, g_lo = _split_hi_lo(gk2)
    gg = _dot(tri_bf, g_hi) + _dot(tri_bf, g_lo)

    def rows2(a0, a1):
        n = a0.shape[1]
        return jnp.concatenate([jnp.broadcast_to(a0, (qc, n)), jnp.broadcast_to(a1, (qc, n))], axis=0)

    mid = qc // 2 if rev else qc // 2 - 1
    last = 0 if rev else qc - 1
    gmid = rows2(gg[mid:mid + 1, :], gg[qc + mid:qc + mid + 1, :])
    gl0 = gg[last:last + 1, :]
    gl1 = gg[qc + last:qc + last + 1, :]
    glast = rows2(gl0, gl1)
    dec0 = jnp.exp2(gl0)
    dec1 = jnp.exp2(gl1)
    dec_tot = jnp.exp2(gl0 + gl1)

    qe_f = q_ref[rows, :] * jnp.exp2(gg - (gmid + 0.5 * jnp.log2(float(GLA_HEAD_K))))
    ke_f = k_ref[rows, :] * jnp.exp2(gmid - gg)
    qe = qe_f.astype(BF16)
    ke = ke_f.astype(BF16)
    qin = qe_f * jnp.exp2(gmid)
    kend = ke_f * jnp.exp2(glast - gmid)
    if rev:
        qin_x = jnp.concatenate([qin[:qc] * dec1, qin[qc:]], axis=0)
        kend_x = jnp.concatenate([kend[:qc], kend[qc:] * dec0], axis=0)
    else:
        qin_x = jnp.concatenate([qin[:qc], qin[qc:] * dec0], axis=0)
        kend_x = jnp.concatenate([kend[:qc] * dec1, kend[qc:]], axis=0)
    return dict(
        qe=qe, ke=ke, qin=qin.astype(BF16), kend=kend.astype(BF16),
        qin_x=qin_x.astype(BF16), kend_x=kend_x.astype(BF16),
        vb=v_ref[rows, :].astype(BF16),
        decb=jnp.broadcast_to(dec_tot, (LANES, D_GLA_K)),
        diag=diag_f > 0.5,
        offd=(c_blk - r_blk == 1) if rev else (r_blk - c_blk == 1))


def _gla_head(h, ctx, s_ref):
    ks = slice(h * GLA_HEAD_K, (h + 1) * GLA_HEAD_K)
    vs = slice(h * GLA_HEAD_V, (h + 1) * GLA_HEAD_V)
    a_diag = _dot_nt(ctx["qe"][:, ks], ctx["ke"][:, ks])
    a_off = _dot_nt(ctx["qin"][:, ks], ctx["kend"][:, ks])
    a = jnp.where(ctx["diag"], a_diag, jnp.where(ctx["offd"], a_off, 0.0)).astype(BF16)
    s = s_ref[h]
    vb_h = ctx["vb"][:, vs]
    o_h = _dot(a, vb_h) + _dot(ctx["qin_x"][:, ks], s.astype(BF16))
    upd = _dot_tn(ctx["kend_x"][:, ks], vb_h)
    decb = ctx["decb"]
    dcol = jnp.concatenate(
        [decb[:, h * GLA_HEAD_K + t * LANES:h * GLA_HEAD_K + (t + 1) * LANES].T for t in range(GLA_HEAD_K // LANES)],
        axis=0)
    s_ref[h] = s * jnp.concatenate([dcol] * (GLA_HEAD_V // LANES), axis=1) + upd
    return o_h


def _chunk_position(rev, cps):
    step = pl.program_id(0)
    c = (pl.num_programs(0) - 1 - step) if rev else step
    pos = c % cps
    first = pos == ((cps - 1) if rev else 0)
    return pos, first


def _mix_fwd_kernel(xs_m, xs_p, xs_n, bc_m, bc_p, bc_n, small_ref, q_ref, k_ref, v_ref,
                    cwx_ref, cbx_ref, cwbc_ref, cbbc_ref, dtb_ref, alog_ref, e_ref, wg_ref, bg_ref,
                    yf_ref, xsc_ref, bcc_ref, of_ref,
                    s_ssd, s_gla, ext_x, ext_bc, *, cps):
    pos, first = _chunk_position(False, cps)

    @pl.when(first)
    def _():
        s_ssd[...] = jnp.zeros_like(s_ssd)
        s_gla[...] = jnp.zeros_like(s_gla)

    has_prev = pos != 0
    has_next = pos != cps - 1
    xs = _conv_silu(ext_x, xs_m, xs_p, xs_n, cwx_ref, cbx_ref, has_prev, has_next)
    bc = _conv_silu(ext_bc, bc_m, bc_p, bc_n, cwbc_ref, cbbc_ref, has_prev, has_next).astype(BF16)
    xsc_ref[...] = xs
    bcc_ref[...] = bc
    ng = SSD_GROUPS * SSD_STATE
    gw = D_SSD // SSD_GROUPS
    for sc in range(MIX_CHUNKS_FWD):
        rows = slice(sc * SSD_CHUNK, (sc + 1) * SSD_CHUNK)
        sctx = _ssd_pre(small_ref, rows, dtb_ref, alog_ref, False)
        gctx = _gla_pre(q_ref, k_ref, v_ref, small_ref, rows, wg_ref, bg_ref, False)
        for i in range(SSD_GROUPS):
            of_ref[rows, i * GLA_HEAD_V:(i + 1) * GLA_HEAD_V] = _gla_head(i, gctx, s_gla)
            yf_ref[rows, i * gw:(i + 1) * gw] = _ssd_group(
                i, sctx, xs[rows, :], bc[rows, :ng], bc[rows, ng:], e_ref, s_ssd, False)


def _mix_bwd_kernel(xsc_ref, bcc_ref, small_ref, z_ref, yf_ref, q_ref, k_ref, v_ref, gout_ref, of_ref,
                    dtb_ref, alog_ref, e_ref, dskip_ref, snw_ref, wg_ref, bg_ref, gnw_ref,
                    y_ref, o_ref,
                    s_ssd, s_gla, *, cps):
    _, first = _chunk_position(True, cps)

    @pl.when(first)
    def _():
        s_ssd[...] = jnp.zeros_like(s_ssd)
        s_gla[...] = jnp.zeros_like(s_gla)

    ng = SSD_GROUPS * SSD_STATE
    gw = D_SSD // SSD_GROUPS
    for sc in reversed(range(MIX_CHUNKS_BWD)):
        rows = slice(sc * SSD_CHUNK, (sc + 1) * SSD_CHUNK)
        sctx = _ssd_pre(small_ref, rows, dtb_ref, alog_ref, True)
        gctx = _gla_pre(q_ref, k_ref, v_ref, small_ref, rows, wg_ref, bg_ref, True)
        xs = xsc_ref[rows, :]
        bc = bcc_ref[rows, :]
        for i in range(SSD_GROUPS):
            vs = slice(i * GLA_HEAD_V, (i + 1) * GLA_HEAD_V)
            o_h = _rms_normalize(_gla_head(i, gctx, s_gla) + of_ref[rows, vs], gnw_ref[...])
            gz = gout_ref[rows, vs]
            o_ref[rows, vs] = (o_h * _silu(gz)).astype(o_ref.dtype)

            gs = slice(i * gw, (i + 1) * gw)
            y = _ssd_group(i, sctx, xs, bc[:, :ng], bc[:, ng:], e_ref, s_ssd, True)
            y = y + yf_ref[rows, gs] + dskip_ref[:, gs] * xs[:, gs]
            zz = z_ref[rows, gs]
            y = y * _silu(zz)
            y_ref[rows, gs] = _rms_normalize(y, snw_ref[:, gs]).astype(y_ref.dtype)


def _const_spec(a):
    nd = a.ndim
    return pl.BlockSpec(a.shape, lambda s: (0,) * nd)


_MIX_SCRATCH = [
    pltpu.VMEM((SSD_GROUPS, SSD_STATE, D_SSD // SSD_GROUPS), F32),
    pltpu.VMEM((GLA_HEADS, GLA_HEAD_K, GLA_HEAD_V), F32),
]


def _mix_fwd(proj, small, prm, cps):
    m = proj.shape[0]
    q = MIX_CHUNKS_FWD * SSD_CHUNK
    nc = m // q
    rb = q // HALO
    last_hb = m // HALO - 1

    def main(colblk):
        return lambda s: (s, colblk)

    def prev(colblk):
        return lambda s: (jnp.maximum(s * rb - 1, 0), colblk)

    def nxt(colblk):
        return lambda s: (jnp.minimum((s + 1) * rb, last_hb), colblk)

    xs_blk, bc_blk = COL_XS // 2048, COL_BC // 1024
    in_specs = [
        pl.BlockSpec((q, 2048), main(xs_blk)),
        pl.BlockSpec((HALO, 2048), prev(xs_blk)),
        pl.BlockSpec((HALO, 2048), nxt(xs_blk)),
        pl.BlockSpec((q, 1024), main(bc_blk)),
        pl.BlockSpec((HALO, 1024), prev(bc_blk)),
        pl.BlockSpec((HALO, 1024), nxt(bc_blk)),
        pl.BlockSpec((q, SMALL_W), main(0)),
        pl.BlockSpec((q, D_GLA_K), main(COL_Q // D_GLA_K)),
        pl.BlockSpec((q, D_GLA_K), main(COL_K // D_GLA_K)),
        pl.BlockSpec((q, D_GLA_V), main(COL_V // D_GLA_V)),
    ]
    params = [prm["cw_x"], prm["cb_x"], prm["cw_bc"], prm["cb_bc"], prm["dtb_f"], prm["alog_f"], prm["expand"],
              prm["gate_w_f"], prm["gate_b_f"]]
    in_specs += [_const_spec(a) for a in params]
    return pl.pallas_call(
        functools.partial(_mix_fwd_kernel, cps=cps),
        grid=(nc,),
        in_specs=in_specs,
        out_specs=[
            pl.BlockSpec((q, D_SSD), main(0)),
            pl.BlockSpec((q, D_SSD), main(0)),
            pl.BlockSpec((q, 2 * SSD_GROUPS * SSD_STATE), main(0)),
            pl.BlockSpec((q, D_GLA_V), main(0)),
        ],
        out_shape=[
            jax.ShapeDtypeStruct((m, D_SSD), F32),
            jax.ShapeDtypeStruct((m, D_SSD), F32),
            jax.ShapeDtypeStruct((m, 2 * SSD_GROUPS * SSD_STATE), BF16),
            jax.ShapeDtypeStruct((m, D_GLA_V), F32),
        ],
        scratch_shapes=_MIX_SCRATCH + [
            pltpu.VMEM((D_SSD // LANES, q + 2 * HALO, LANES), F32),
            pltpu.VMEM((2 * SSD_GROUPS * SSD_STATE // LANES, q + 2 * HALO, LANES), F32),
        ],
        compiler_params=_cparams(("arbitrary",)),
        name="mix_fwd",
    )(*([proj] * 6), small, *([proj] * 3), *params)


def _mix_bwd(proj, small, y_f, xs_c, bc_c, o_f, prm, cps):
    m = proj.shape[0]
    q = MIX_CHUNKS_BWD * SSD_CHUNK
    nc = m // q

    def main(colblk):
        return lambda s: (nc - 1 - s, colblk)

    in_specs = [
        pl.BlockSpec((q, D_SSD), main(0)),
        pl.BlockSpec((q, 2 * SSD_GROUPS * SSD_STATE), main(0)),
        pl.BlockSpec((q, SMALL_W), main(0)),
        pl.BlockSpec((q, 2048), main(COL_Z // 2048)),
        pl.BlockSpec((q, D_SSD), main(0)),
        pl.BlockSpec((q, D_GLA_K), main(COL_Q // D_GLA_K)),
        pl.BlockSpec((q, D_GLA_K), main(COL_K // D_GLA_K)),
        pl.BlockSpec((q, D_GLA_V), main(COL_V // D_GLA_V)),
        pl.BlockSpec((q, D_GLA_V), main(COL_GOUT // D_GLA_V)),
        pl.BlockSpec((q, D_GLA_V), main(0)),
    ]
    params = [prm["dtb_b"], prm["alog_b"], prm["expand"], prm["dskip_x"], prm["ssd_nw"],
              prm["gate_w_b"], prm["gate_b_b"], prm["gla_nw"]]
    in_specs += [_const_spec(a) for a in params]
    return pl.pallas_call(
        functools.partial(_mix_bwd_kernel, cps=cps),
        grid=(nc,),
        in_specs=in_specs,
        out_specs=[pl.BlockSpec((q, D_SSD), main(0)), pl.BlockSpec((q, D_GLA_V), main(0))],
        out_shape=[jax.ShapeDtypeStruct((m, D_SSD), BF16), jax.ShapeDtypeStruct((m, D_GLA_V), BF16)],
        scratch_shapes=_MIX_SCRATCH,
        compiler_params=_cparams(("arbitrary",)),
        name="mix_bwd",
    )(xs_c, bc_c, small, proj, y_f, proj, proj, proj, proj, o_f, *params)


def _out_proj_kernel(x_ref, ya_ref, yb_ref, wa_ref, wb_ref, o_ref):
    o_ref[...] = x_ref[...] + _dot(ya_ref[...], wa_ref[...]) + _dot(yb_ref[...], wb_ref[...])


def _out_proj(x, ya, yb, wa, wb):
    m, d = x.shape
    ka = ya.shape[1]
    kb = yb.shape[1]
    tm, tn = TM_OUT, TN_OUT
    return pl.pallas_call(
        _out_proj_kernel,
        grid=(m // tm, d // tn),
        in_specs=[
            pl.BlockSpec((tm, tn), lambda i, j: (i, j)),
            pl.BlockSpec((tm, ka), lambda i, j: (i, 0)),
            pl.BlockSpec((tm, kb), lambda i, j: (i, 0)),
            pl.BlockSpec((ka, tn), lambda i, j: (0, j)),
            pl.BlockSpec((kb, tn), lambda i, j: (0, j)),
        ],
        out_specs=pl.BlockSpec((tm, tn), lambda i, j: (i, j)),
        out_shape=jax.ShapeDtypeStruct((m, d), F32),
        compiler_params=_cparams(("parallel", "arbitrary")),
        name="out_proj",
    )(x, ya, yb, wa, wb)


def _xattn_kernel(x_ref, nw_ref, wq_ref, kv_ref, wo_ref, o_ref):
    x = x_ref[...]
    h = _rms_normalize(x, nw_ref[...]).astype(BF16)
    qq = _dot(h, wq_ref[...]).astype(BF16)
    scale = XATTN_HEAD_DIM ** -0.5
    heads = []
    for hd in range(XATTN_HEADS):
        ds = slice(hd * XATTN_HEAD_DIM, (hd + 1) * XATTN_HEAD_DIM)
        kh = kv_ref[:, ds]
        vh = kv_ref[:, D_MODEL + hd * XATTN_HEAD_DIM:D_MODEL + (hd + 1) * XATTN_HEAD_DIM]
        s = _dot_nt(qq[:, ds], kh) * scale
        s = s - jnp.max(s, axis=-1, keepdims=True)
        e = jnp.exp(s)
        p = e / jnp.sum(e, axis=-1, keepdims=True)
        heads.append(_dot(p.astype(BF16), vh).astype(BF16))
    o = jnp.concatenate(heads, axis=1)
    o_ref[...] = x + _dot(o, wo_ref[...])


def _xattn(x, nw, wq, kv, wo, tiles_per_seq):
    m, d = x.shape
    tm = TM_XATTN
    single = pl.Buffered(1)
    return pl.pallas_call(
        _xattn_kernel,
        grid=(m // tm,),
        in_specs=[
            pl.BlockSpec((tm, d), lambda i: (i, 0)),
            pl.BlockSpec((1, d), lambda i: (0, 0)),
            pl.BlockSpec((d, d), lambda i: (0, 0), pipeline_mode=single),
            pl.BlockSpec((N_MEM, 2 * d), lambda i: (i // tiles_per_seq, 0)),
            pl.BlockSpec((d, d), lambda i: (0, 0), pipeline_mode=single),
        ],
        out_specs=pl.BlockSpec((tm, d), lambda i: (i, 0)),
        out_shape=jax.ShapeDtypeStruct((m, d), F32),
        compiler_params=_cparams(("arbitrary",)),
        name="xattn",
    )(x, nw, wq, kv, wo)


_W_IN_SEGMENTS = (
    (COL_Z, 0, 4096),
    (COL_V, 7232, 2048),
    (COL_GOUT, 9312, 2048),
    (COL_BC, 4096, 1024),
    (COL_Q, 5184, 1024),
    (COL_K, 6208, 1024),
)
_W_IN_DT = (5120, 5184)
_W_IN_LOW = (9280, 9312)
W_RELAYOUT_ROWS = 256


def _w_in_relayout_kernel(w_ref, main_ref, small_ref):
    for dst, src, width in _W_IN_SEGMENTS:
        main_ref[:, dst:dst + width] = w_ref[:, src:src + width].astype(BF16)
    rows = w_ref.shape[0]
    narrow = _W_IN_DT[1] - _W_IN_DT[0] + _W_IN_LOW[1] - _W_IN_LOW[0]
    small_ref[...] = jnp.concatenate(
        [w_ref[:, _W_IN_DT[0]:_W_IN_DT[1]], w_ref[:, _W_IN_LOW[0]:_W_IN_LOW[1]],
         jnp.zeros((rows, SMALL_W - narrow), F32)], axis=1).astype(BF16)


def _relayout_w_in(w_in):
    _, d, n_in = w_in.shape
    tr = W_RELAYOUT_ROWS
    return pl.pallas_call(
        _w_in_relayout_kernel,
        grid=(d // tr,),
        in_specs=[pl.BlockSpec((None, tr, n_in), lambda i: (0, i, 0))],
        out_specs=[pl.BlockSpec((tr, N_PROJ), lambda i: (i, 0)), pl.BlockSpec((tr, SMALL_W), lambda i: (i, 0))],
        out_shape=[jax.ShapeDtypeStruct((d, N_PROJ), BF16), jax.ShapeDtypeStruct((d, SMALL_W), BF16)],
        compiler_params=_cparams(("arbitrary",)),
        name="w_in_relayout",
    )(w_in)


def _prepare(p):
    w_main, w_small = _relayout_w_in(p["w_in"])

    def gate_w(w, row0):
        full = jnp.zeros((SMALL_W, D_GLA_K), F32)
        return full.at[row0:row0 + GLA_RANK].set(w).astype(BF16)

    head_of_lane = jnp.arange(D_SSD) // SSD_HEADDIM
    rows = jnp.arange(4 * SSD_HEADS)[:, None]
    expand = jnp.stack([(rows == (head_of_lane[None, :] + SSD_HEADS * t)) for t in range(3)], axis=0)
    conv_w, conv_b = p["conv_w"], p["conv_b"]
    return dict(
        w_main=w_main, w_small=w_small,
        cw_x=conv_w[:, :D_SSD], cb_x=conv_b[None, :D_SSD],
        cw_bc=conv_w[:, D_SSD:], cb_bc=conv_b[None, D_SSD:],
        dtb_f=p["dt_bias_fwd"][:, None], dtb_b=p["dt_bias_bwd"][:, None],
        alog_f=p["a_log_fwd"][:, None], alog_b=p["a_log_bwd"][:, None],
        expand=expand.astype(BF16),
        dskip_x=jnp.repeat(p["d_skip"], SSD_HEADDIM)[None, :],
        ssd_nw=p["ssd_norm"][None, :],
        gate_w_f=gate_w(p["gla_gate_w_fwd"], 2 * SSD_HEADS),
        gate_w_b=gate_w(p["gla_gate_w_bwd"], 2 * SSD_HEADS + GLA_RANK),
        gate_b_f=p["gla_gate_b_fwd"][None, :], gate_b_b=p["gla_gate_b_bwd"][None, :],
        gla_nw=p["gla_norm"][None, :],
        w_out_a=p["w_out"][:D_SSD].astype(BF16), w_out_b=p["w_out"][D_SSD:].astype(BF16),
        ffn1=(p["ffn1_norm"][None, :], p["ffn1_w1"].astype(BF16), p["ffn1_w3"].astype(BF16), p["ffn1_w2"].astype(BF16)),
        ffn2=(p["ffn2_norm"][None, :], p["ffn2_w1"].astype(BF16), p["ffn2_w3"].astype(BF16), p["ffn2_w2"].astype(BF16)),
        mix_nw=p["mix_norm"][None, :],
        xattn_nw=p["xattn_norm"][None, :], mem_nw=p["mem_norm"][None, :],
        w_cq=p["w_cq"].astype(BF16), w_ckv=p["w_ckv"].astype(BF16), w_co=p["w_co"].astype(BF16),
        final_nw=p["final_norm"][None, :],
    )


def _trunk(x3, mem3, prm):
    b, l, d = x3.shape
    x = x3.reshape(b * l, d)
    mem = mem3.reshape(b * N_MEM, d)
    x = _ffn(x, *prm["ffn1"])
    proj, small = _in_proj(x, prm["mix_nw"], prm["w_main"], prm["w_small"])
    y_f, xs_c, bc_c, o_f = _mix_fwd(proj, small, prm, l // (MIX_CHUNKS_FWD * SSD_CHUNK))
    y, o = _mix_bwd(proj, small, y_f, xs_c, bc_c, o_f, prm, l // (MIX_CHUNKS_BWD * SSD_CHUNK))
    x = _out_proj(x, y, o, prm["w_out_a"], prm["w_out_b"])
    kv = _norm_matmul(mem, prm["mem_nw"], prm["w_ckv"], BF16, N_MEM, 1024, "kv_proj")
    x = _xattn(x, prm["xattn_nw"], prm["w_cq"], kv, prm["w_co"], l // TM_XATTN)
    x = _ffn(x, *prm["ffn2"], final_nw=prm["final_nw"])
    return x.reshape(b, l, d)


def kernel(x_prompt, x_sample, mem_prompt, mem_sample, ffn1_norm, ffn1_w1, ffn1_w3, ffn1_w2, mix_norm, w_in, conv_w, conv_b, dt_bias_fwd, dt_bias_bwd, a_log_fwd, a_log_bwd, d_skip, ssd_norm, gla_gate_w_fwd, gla_gate_b_fwd, gla_gate_w_bwd, gla_gate_b_bwd, gla_norm, w_out, xattn_norm, mem_norm, w_cq, w_ckv, w_co, ffn2_norm, ffn2_w1, ffn2_w3, ffn2_w2, final_norm):
    p = dict(
        ffn1_norm=ffn1_norm[0], ffn1_w1=ffn1_w1[0], ffn1_w3=ffn1_w3[0], ffn1_w2=ffn1_w2[0],
        mix_norm=mix_norm[0], w_in=w_in, conv_w=conv_w[0], conv_b=conv_b[0],
        dt_bias_fwd=dt_bias_fwd[0], dt_bias_bwd=dt_bias_bwd[0], a_log_fwd=a_log_fwd[0], a_log_bwd=a_log_bwd[0],
        d_skip=d_skip[0], ssd_norm=ssd_norm[0],
        gla_gate_w_fwd=gla_gate_w_fwd[0], gla_gate_b_fwd=gla_gate_b_fwd[0],
        gla_gate_w_bwd=gla_gate_w_bwd[0], gla_gate_b_bwd=gla_gate_b_bwd[0],
        gla_norm=gla_norm[0], w_out=w_out[0], xattn_norm=xattn_norm[0], mem_norm=mem_norm[0],
        w_cq=w_cq[0], w_ckv=w_ckv[0], w_co=w_co[0],
        ffn2_norm=ffn2_norm[0], ffn2_w1=ffn2_w1[0], ffn2_w3=ffn2_w3[0], ffn2_w2=ffn2_w2[0],
        final_norm=final_norm,
    )
    prm = _prepare(p)
    return (_trunk(x_prompt, mem_prompt, prm), _trunk(x_sample, mem_sample, prm))
```

```python
import functools

import jax
import jax.numpy as jnp
from jax import lax
from jax.experimental import pallas as pl
from jax.experimental.pallas import tpu as pltpu

F32 = jnp.float32
BF16 = jnp.bfloat16

D_MODEL = 2048
N_MEM = 256
D_SSD = 2048
SSD_HEADS = 32
SSD_HEADDIM = 64
SSD_GROUPS = 4
SSD_STATE = 128
SSD_CHUNK = 128
CONV_WIDTH = 5
GLA_HEADS = 4
GLA_HEAD_K = 256
GLA_HEAD_V = 512
D_GLA_K = 1024
D_GLA_V = 2048
GLA_RANK = 16
GLA_NORMALIZER = 16.0
GLA_CHUNK = 64
XATTN_HEADS = 4
XATTN_HEAD_DIM = 512
D_FF = 5632
EPS = 1e-6

COL_Z = 0
COL_XS = 2048
COL_V = 4096
COL_GOUT = 6144
COL_BC = 8192
COL_Q = 9216
COL_K = 10240
N_PROJ = 11264
SMALL_W = 128

SUBLANE = 8
LANES = 128
HALO = SUBLANE
TM_FFN = 1024
TF_FFN = 512
FFN_ROW_CHUNK = 256
FFN_COL_CHUNK = 512
TM_PROJ = 1024
TN_PROJ = 1024
PROJ_ROW_CHUNK = 256
PROJ_COL_CHUNK = 512
TM_OUT = 1024
TN_OUT = 512
TM_XATTN = 512
GLA_BLOCK = SSD_CHUNK
MIX_CHUNKS_FWD = 2
MIX_CHUNKS_BWD = 1
LOG2E = 1.4426950408889634
VMEM_LIMIT = 56 * 1024 * 1024
VMEM_LIMIT_FFN = 62 * 1024 * 1024


def _cparams(sem, vmem_limit=VMEM_LIMIT):
    return pltpu.CompilerParams(dimension_semantics=sem, vmem_limit_bytes=vmem_limit)


def _rms_normalize(x, w):
    ms = jnp.mean(x * x, axis=-1, keepdims=True)
    return x * lax.rsqrt(ms + EPS) * w


def _silu(x):
    h = 0.5 * x
    return h + h * jnp.tanh(h)


def _softplus(x):
    return jnp.maximum(x, 0.0) + jnp.log(1.0 + jnp.exp(-jnp.abs(x)))


def _split_hi_lo(x):
    hi = x.astype(BF16)
    lo = (x - hi.astype(F32)).astype(BF16)
    return hi, lo


def _dot(a, b):
    return jnp.dot(a, b, preferred_element_type=F32)


def _dot_nt(a, b):
    return lax.dot_general(a, b, (((1,), (1,)), ((), ())), preferred_element_type=F32)


def _dot_tn(a, b):
    return lax.dot_general(a, b, (((0,), (0,)), ((), ())), preferred_element_type=F32)


def _ffn_kernel(x_ref, nw_ref, w1_ref, w3_ref, w2_ref, *rest, final):
    if final:
        fnw_ref, o_ref, h_ref = rest
    else:
        o_ref, h_ref = rest
    j = pl.program_id(1)

    tm, d = o_ref.shape

    def for_row_chunks(body):
        def step(r, carry):
            body(pl.ds(pl.multiple_of(r * FFN_ROW_CHUNK, FFN_ROW_CHUNK), FFN_ROW_CHUNK))
            return carry
        lax.fori_loop(0, tm // FFN_ROW_CHUNK, step, 0)

    @pl.when(j == 0)
    def _():
        def body(rs):
            h_ref[rs, :] = _rms_normalize(x_ref[rs, :], nw_ref[...]).astype(BF16)
            o_ref[rs, :] = jnp.zeros((FFN_ROW_CHUNK, d), F32)
        for_row_chunks(body)

    h = h_ref[...]
    g = _dot(h, w1_ref[...])
    u = _dot(h, w3_ref[...])
    a = (g * jax.nn.sigmoid(g) * u).astype(BF16)
    for c in range(0, d, FFN_COL_CHUNK):
        cs = slice(c, c + FFN_COL_CHUNK)
        o_ref[:, cs] += _dot(a, w2_ref[:, cs])

    @pl.when(j == pl.num_programs(1) - 1)
    def _():
        def body(rs):
            y = x_ref[rs, :] + 0.5 * o_ref[rs, :]
            if final:
                y = _rms_normalize(y, fnw_ref[...])
            o_ref[rs, :] = y
        for_row_chunks(body)


def _ffn(x, nw, w1, w3, w2, final_nw=None):
    m, d = x.shape
    dff = w1.shape[1]
    tm, tf = TM_FFN, TF_FFN
    final = final_nw is not None
    in_specs = [
        pl.BlockSpec((tm, d), lambda i, j: (i, 0)),
        pl.BlockSpec((1, d), lambda i, j: (0, 0)),
        pl.BlockSpec((d, tf), lambda i, j: (0, j)),
        pl.BlockSpec((d, tf), lambda i, j: (0, j)),
        pl.BlockSpec((tf, d), lambda i, j: (j, 0)),
    ]
    args = [x, nw, w1, w3, w2]
    if final:
        in_specs.append(pl.BlockSpec((1, d), lambda i, j: (0, 0)))
        args.append(final_nw)
    return pl.pallas_call(
        functools.partial(_ffn_kernel, final=final),
        grid=(m // tm, dff // tf),
        in_specs=in_specs,
        out_specs=pl.BlockSpec((tm, d), lambda i, j: (i, 0)),
        out_shape=jax.ShapeDtypeStruct((m, d), F32),
        scratch_shapes=[pltpu.VMEM((tm, d), BF16)],
        compiler_params=_cparams(("parallel", "arbitrary"), VMEM_LIMIT_FFN),
        name="ffn_final" if final else "ffn",
    )(*args)


def _norm_matmul_kernel(x_ref, nw_ref, w_ref, o_ref, h_ref):
    @pl.when(pl.program_id(1) == 0)
    def _():
        h_ref[...] = _rms_normalize(x_ref[...], nw_ref[...]).astype(BF16)

    o_ref[...] = _dot(h_ref[...], w_ref[...]).astype(o_ref.dtype)


def _norm_matmul(x, nw, w, out_dtype, tm, tn, name):
    m, d = x.shape
    n = w.shape[1]
    return pl.pallas_call(
        _norm_matmul_kernel,
        grid=(m // tm, n // tn),
        in_specs=[
            pl.BlockSpec((tm, d), lambda i, j: (i, 0)),
            pl.BlockSpec((1, d), lambda i, j: (0, 0)),
            pl.BlockSpec((d, tn), lambda i, j: (0, j)),
        ],
        out_specs=pl.BlockSpec((tm, tn), lambda i, j: (i, j)),
        out_shape=jax.ShapeDtypeStruct((m, n), out_dtype),
        scratch_shapes=[pltpu.VMEM((tm, d), BF16)],
        compiler_params=_cparams(("parallel", "arbitrary")),
        name=name,
    )(x, nw, w)


def _in_proj_kernel(x_ref, nw_ref, w_ref, ws_ref, o_ref, small_ref, h_ref):
    tm = h_ref.shape[0]

    @pl.when(pl.program_id(1) == 0)
    def _():
        def step(r, carry):
            rs = pl.ds(pl.multiple_of(r * PROJ_ROW_CHUNK, PROJ_ROW_CHUNK), PROJ_ROW_CHUNK)
            h_ref[rs, :] = _rms_normalize(x_ref[rs, :], nw_ref[...]).astype(BF16)
            return carry
        lax.fori_loop(0, tm // PROJ_ROW_CHUNK, step, 0)
        small_ref[...] = _dot(h_ref[...], ws_ref[...])

    h = h_ref[...]
    for c in range(0, o_ref.shape[1], PROJ_COL_CHUNK):
        cs = slice(c, c + PROJ_COL_CHUNK)
        o_ref[:, cs] = _dot(h, w_ref[:, cs])


def _in_proj(x, nw, w_main, w_small):
    m, d = x.shape
    n = w_main.shape[1]
    tm, tn = TM_PROJ, TN_PROJ
    return pl.pallas_call(
        _in_proj_kernel,
        grid=(m // tm, n // tn),
        in_specs=[
            pl.BlockSpec((tm, d), lambda i, j: (i, 0)),
            pl.BlockSpec((1, d), lambda i, j: (0, 0)),
            pl.BlockSpec((d, tn), lambda i, j: (0, j)),
            pl.BlockSpec((d, SMALL_W), lambda i, j: (0, 0)),
        ],
        out_specs=[
            pl.BlockSpec((tm, tn), lambda i, j: (i, j)),
            pl.BlockSpec((tm, SMALL_W), lambda i, j: (i, 0)),
        ],
        out_shape=[jax.ShapeDtypeStruct((m, n), F32), jax.ShapeDtypeStruct((m, SMALL_W), F32)],
        scratch_shapes=[pltpu.VMEM((tm, d), BF16)],
        compiler_params=_cparams(("parallel", "arbitrary")),
        name="in_proj",
    )(x, nw, w_main, w_small)


def _conv_silu(ext_ref, m_ref, p_ref, n_ref, w_ref, b_ref, has_prev, has_next):
    q = m_ref.shape[0]
    nslab = m_ref.shape[1] // LANES
    for s in range(nslab):
        ls = slice(s * LANES, (s + 1) * LANES)
        ext_ref[s, 0:HALO, :] = jnp.where(has_prev, p_ref[:, ls], 0.0)
        ext_ref[s, HALO:HALO + q, :] = m_ref[:, ls]
        ext_ref[s, HALO + q:HALO + q + HALO, :] = jnp.where(has_next, n_ref[:, ls], 0.0)
    outs = []
    for s in range(nslab):
        ls = slice(s * LANES, (s + 1) * LANES)
        acc = b_ref[:, ls]
        for t in range(CONV_WIDTH):
            r0 = HALO + t - CONV_WIDTH // 2
            acc = acc + ext_ref[s, pl.ds(r0, q, stride=1), :] * w_ref[t:t + 1, ls]
        outs.append(_silu(acc))
    return jnp.concatenate(outs, axis=1)


def _ssd_pre(small_ref, rows, dtb_ref, alog_ref, rev):
    q = SSD_CHUNK
    off = SSD_HEADS if rev else 0
    sm_t = small_ref[rows, :].T
    dt_t = _softplus(sm_t[off:off + SSD_HEADS, :] + dtb_ref[...])
    a_t = dt_t * (-jnp.exp(alog_ref[...]))
    r_i = lax.broadcasted_iota(jnp.int32, (q, q), 0)
    c_i = lax.broadcasted_iota(jnp.int32, (q, q), 1)
    tri = (r_i >= c_i) if rev else (r_i <= c_i)
    tri_bf = jnp.where(tri, 1.0, 0.0).astype(BF16)
    a_hi, a_lo = _split_hi_lo(a_t)
    cum_t = _dot(a_hi, tri_bf) + _dot(a_lo, tri_bf)
    tot_t = cum_t[:, 0:1] if rev else cum_t[:, q - 1:q]
    wend_t = dt_t * jnp.exp(tot_t - cum_t)
    dec_t = jnp.exp(cum_t)
    cum2_t = cum_t * LOG2E
    vt = jnp.concatenate([dt_t, wend_t, dec_t, cum2_t], axis=0)
    vv = vt.T
    lane = lax.broadcasted_iota(jnp.int32, (q, LANES), 1)
    return dict(
        vv=vv, vv_bf=vv.astype(BF16), cum2_t=cum2_t,
        causal=(c_i >= r_i) if rev else (c_i <= r_i),
        lo_half=lane < SSD_HEADDIM)


def _ssd_group(g, ctx, xs, bm, cm, e_ref, s_ref, rev):
    q = SSD_CHUNK
    hpg = SSD_HEADS // SSD_GROUPS
    gw = hpg * SSD_HEADDIM
    gs = slice(g * gw, (g + 1) * gw)
    vv, vv_bf, cum2_t = ctx["vv"], ctx["vv_bf"], ctx["cum2_t"]
    xs_g = xs[:, gs]
    xdt = (xs_g * _dot(vv_bf, e_ref[0, :, gs])).astype(BF16)
    xend = (xs_g * _dot(vv_bf, e_ref[1, :, gs])).astype(BF16)
    dec_x = _dot(vv_bf, e_ref[2, :, gs])
    dec_tot = dec_x[0:1, :] if rev else dec_x[q - 1:q, :]
    cm_g = cm[:, g * SSD_STATE:(g + 1) * SSD_STATE]
    bm_g = bm[:, g * SSD_STATE:(g + 1) * SSD_STATE]
    cb = _dot_nt(cm_g, bm_g)
    y_off = _dot(cm_g, s_ref[g].astype(BF16)) * dec_x
    y_pairs = []
    for p in range(hpg // 2):
        ws = []
        for hh in range(2):
            h = g * hpg + 2 * p + hh
            col = 3 * SSD_HEADS + h
            seg = vv[:, col:col + 1] - cum2_t[h:h + 1, :]
            ws.append((cb * jnp.exp2(jnp.where(ctx["causal"], seg, -jnp.inf))).astype(BF16))
        lhs = jnp.concatenate(ws, axis=1)
        xp = xdt[:, p * LANES:(p + 1) * LANES]
        zero = jnp.zeros_like(xp)
        rhs = jnp.concatenate([jnp.where(ctx["lo_half"], xp, zero), jnp.where(ctx["lo_half"], zero, xp)], axis=0)
        y_pairs.append(_dot(lhs, rhs))
    s_ref[g] = s_ref[g] * dec_tot + _dot_tn(bm_g, xend)
    return jnp.concatenate(y_pairs, axis=1) + y_off


def _gla_pre(q_ref, k_ref, v_ref, small_ref, rows, wg_ref, bg_ref, rev):
    tb, qc = GLA_BLOCK, GLA_CHUNK
    r_i = lax.broadcasted_iota(jnp.int32, (tb, tb), 0)
    c_i = lax.broadcasted_iota(jnp.int32, (tb, tb), 1)
    r_blk = r_i // qc
    c_blk = c_i // qc
    order = (c_i >= r_i) if rev else (c_i <= r_i)
    diag_f = jnp.where(r_blk == c_blk, jnp.where(order, 1.0, 0.0), 0.0)
    tri_bf = diag_f.astype(BF16)

    low = small_ref[rows, :].astype(BF16)
    pre = _dot(low, wg_ref[...]) + bg_ref[...]
    gk2 = _softplus(-pre) * (-LOG2E / GLA_NORMALIZER)
    g_hi, g_lo = _split_hi_lo(gk2)
    gg = _dot(tri_bf, g_hi) + _dot(tri_bf, g_lo)

    def rows2(a0, a1):
        n = a0.shape[1]
        return jnp.concatenate([jnp.broadcast_to(a0, (qc, n)), jnp.broadcast_to(a1, (qc, n))], axis=0)

    mid = qc // 2 if rev else qc // 2 - 1
    last = 0 if rev else qc - 1
    gmid = rows2(gg[mid:mid + 1, :], gg[qc + mid:qc + mid + 1, :])
    gl0 = gg[last:last + 1, :]
    gl1 = gg[qc + last:qc + last + 1, :]
    glast = rows2(gl0, gl1)
    dec0 = jnp.exp2(gl0)
    dec1 = jnp.exp2(gl1)
    dec_tot = jnp.exp2(gl0 + gl1)

    qe_f = q_ref[rows, :] * jnp.exp2(gg - (gmid + 0.5 * jnp.log2(float(GLA_HEAD_K))))
    ke_f = k_ref[rows, :] * jnp.exp2(gmid - gg)
    qe = qe_f.astype(BF16)
    ke = ke_f.astype(BF16)
    qin = qe_f * jnp.exp2(gmid)
    kend = ke_f * jnp.exp2(glast - gmid)
    if rev:
        qin_x = jnp.concatenate([qin[:qc] * dec1, qin[qc:]], axis=0)
        kend_x = jnp.concatenate([kend[:qc], kend[qc:] * dec0], axis=0)
    else:
        qin_x = jnp.concatenate([qin[:qc], qin[qc:] * dec0], axis=0)
        kend_x = jnp.concatenate([kend[:qc] * dec1, kend[qc:]], axis=0)
    return dict(
        qe=qe, ke=ke, qin=qin.astype(BF16), kend=kend.astype(BF16),
        qin_x=qin_x.astype(BF16), kend_x=kend_x.astype(BF16),
        vb=v_ref[rows, :].astype(BF16),
        decb=jnp.broadcast_to(dec_tot, (LANES, D_GLA_K)),
        diag=diag_f > 0.5,
        offd=(c_blk - r_blk == 1) if rev else (r_blk - c_blk == 1))


def _gla_head(h, ctx, s_ref):
    ks = slice(h * GLA_HEAD_K, (h + 1) * GLA_HEAD_K)
    vs = slice(h * GLA_HEAD_V, (h + 1) * GLA_HEAD_V)
    a_diag = _dot_nt(ctx["qe"][:, ks], ctx["ke"][:, ks])
    a_off = _dot_nt(ctx["qin"][:, ks], ctx["kend"][:, ks])
    a = jnp.where(ctx["diag"], a_diag, jnp.where(ctx["offd"], a_off, 0.0)).astype(BF16)
    s = s_ref[h]
    vb_h = ctx["vb"][:, vs]
    o_h = _dot(a, vb_h) + _dot(ctx["qin_x"][:, ks], s.astype(BF16))
    upd = _dot_tn(ctx["kend_x"][:, ks], vb_h)
    decb = ctx["decb"]
    dcol = jnp.concatenate(
        [decb[:, h * GLA_HEAD_K + t * LANES:h * GLA_HEAD_K + (t + 1) * LANES].T for t in range(GLA_HEAD_K // LANES)],
        axis=0)
    s_ref[h] = s * jnp.concatenate([dcol] * (GLA_HEAD_V // LANES), axis=1) + upd
    return o_h


def _chunk_position(rev, cps):
    step = pl.program_id(0)
    c = (pl.num_programs(0) - 1 - step) if rev else step
    pos = c % cps
    first = pos == ((cps - 1) if rev else 0)
    return pos, first


def _mix_fwd_kernel(xs_m, xs_p, xs_n, bc_m, bc_p, bc_n, small_ref, q_ref, k_ref, v_ref,
                    cwx_ref, cbx_ref, cwbc_ref, cbbc_ref, dtb_ref, alog_ref, e_ref, wg_ref, bg_ref,
                    yf_ref, xsc_ref, bcc_ref, of_ref,
                    s_ssd, s_gla, ext_x, ext_bc, *, cps):
    pos, first = _chunk_position(False, cps)

    @pl.when(first)
    def _():
        s_ssd[...] = jnp.zeros_like(s_ssd)
        s_gla[...] = jnp.zeros_like(s_gla)

    has_prev = pos != 0
    has_next = pos != cps - 1
    xs = _conv_silu(ext_x, xs_m, xs_p, xs_n, cwx_ref, cbx_ref, has_prev, has_next)
    bc = _conv_silu(ext_bc, bc_m, bc_p, bc_n, cwbc_ref, cbbc_ref, has_prev, has_next).astype(BF16)
    xsc_ref[...] = xs
    bcc_ref[...] = bc
    ng = SSD_GROUPS * SSD_STATE
    gw = D_SSD // SSD_GROUPS
    for sc in range(MIX_CHUNKS_FWD):
        rows = slice(sc * SSD_CHUNK, (sc + 1) * SSD_CHUNK)
        sctx = _ssd_pre(small_ref, rows, dtb_ref, alog_ref, False)
        gctx = _gla_pre(q_ref, k_ref, v_ref, small_ref, rows, wg_ref, bg_ref, False)
        for i in range(SSD_GROUPS):
            of_ref[rows, i * GLA_HEAD_V:(i + 1) * GLA_HEAD_V] = _gla_head(i, gctx, s_gla)
            yf_ref[rows, i * gw:(i + 1) * gw] = _ssd_group(
                i, sctx, xs[rows, :], bc[rows, :ng], bc[rows, ng:], e_ref, s_ssd, False)


def _mix_bwd_kernel(xsc_ref, bcc_ref, small_ref, z_ref, yf_ref, q_ref, k_ref, v_ref, gout_ref, of_ref,
                    dtb_ref, alog_ref, e_ref, dskip_ref, snw_ref, wg_ref, bg_ref, gnw_ref,
                    y_ref, o_ref,
                    s_ssd, s_gla, *, cps):
    _, first = _chunk_position(True, cps)

    @pl.when(first)
    def _():
        s_ssd[...] = jnp.zeros_like(s_ssd)
        s_gla[...] = jnp.zeros_like(s_gla)

    ng = SSD_GROUPS * SSD_STATE
    gw = D_SSD // SSD_GROUPS
    for sc in reversed(range(MIX_CHUNKS_BWD)):
        rows = slice(sc * SSD_CHUNK, (sc + 1) * SSD_CHUNK)
        sctx = _ssd_pre(small_ref, rows, dtb_ref, alog_ref, True)
        gctx = _gla_pre(q_ref, k_ref, v_ref, small_ref, rows, wg_ref, bg_ref, True)
        xs = xsc_ref[rows, :]
        bc = bcc_ref[rows, :]
        for i in range(SSD_GROUPS):
            vs = slice(i * GLA_HEAD_V, (i + 1) * GLA_HEAD_V)
            o_h = _rms_normalize(_gla_head(i, gctx, s_gla) + of_ref[rows, vs], gnw_ref[...])
            gz = gout_ref[rows, vs]
            o_ref[rows, vs] = (o_h * _silu(gz)).astype(o_ref.dtype)

            gs = slice(i * gw, (i + 1) * gw)
            y = _ssd_group(i, sctx, xs, bc[:, :ng], bc[:, ng:], e_ref, s_ssd, True)
            y = y + yf_ref[rows, gs] + dskip_ref[:, gs] * xs[:, gs]
            zz = z_ref[rows, gs]
            y = y * _silu(zz)
            y_ref[rows, gs] = _rms_normalize(y, snw_ref[:, gs]).astype(y_ref.dtype)


def _const_spec(a):
    nd = a.ndim
    return pl.BlockSpec(a.shape, lambda s: (0,) * nd)


_MIX_SCRATCH = [
    pltpu.VMEM((SSD_GROUPS, SSD_STATE, D_SSD // SSD_GROUPS), F32),
    pltpu.VMEM((GLA_HEADS, GLA_HEAD_K, GLA_HEAD_V), F32),
]


def _mix_fwd(proj, small, prm, cps):
    m = proj.shape[0]
    q = MIX_CHUNKS_FWD * SSD_CHUNK
    nc = m // q
    rb = q // HALO
    last_hb = m // HALO - 1

    def main(colblk):
        return lambda s: (s, colblk)

    def prev(colblk):
        return lambda s: (jnp.maximum(s * rb - 1, 0), colblk)

    def nxt(colblk):
        return lambda s: (jnp.minimum((s + 1) * rb, last_hb), colblk)

    xs_blk, bc_blk = COL_XS // 2048, COL_BC // 1024
    in_specs = [
        pl.BlockSpec((q, 2048), main(xs_blk)),
        pl.BlockSpec((HALO, 2048), prev(xs_blk)),
        pl.BlockSpec((HALO, 2048), nxt(xs_blk)),
        pl.BlockSpec((q, 1024), main(bc_blk)),
        pl.BlockSpec((HALO, 1024), prev(bc_blk)),
        pl.BlockSpec((HALO, 1024), nxt(bc_blk)),
        pl.BlockSpec((q, SMALL_W), main(0)),
        pl.BlockSpec((q, D_GLA_K), main(COL_Q // D_GLA_K)),
        pl.BlockSpec((q, D_GLA_K), main(COL_K // D_GLA_K)),
        pl.BlockSpec((q, D_GLA_V), main(COL_V // D_GLA_V)),
    ]
    params = [prm["cw_x"], prm["cb_x"], prm["cw_bc"], prm["cb_bc"], prm["dtb_f"], prm["alog_f"], prm["expand"],
              prm["gate_w_f"], prm["gate_b_f"]]
    in_specs += [_const_spec(a) for a in params]
    return pl.pallas_call(
        functools.partial(_mix_fwd_kernel, cps=cps),
        grid=(nc,),
        in_specs=in_specs,
        out_specs=[
            pl.BlockSpec((q, D_SSD), main(0)),
            pl.BlockSpec((q, D_SSD), main(0)),
            pl.BlockSpec((q, 2 * SSD_GROUPS * SSD_STATE), main(0)),
            pl.BlockSpec((q, D_GLA_V), main(0)),
        ],
        out_shape=[
            jax.ShapeDtypeStruct((m, D_SSD), F32),
            jax.ShapeDtypeStruct((m, D_SSD), F32),
            jax.ShapeDtypeStruct((m, 2 * SSD_GROUPS * SSD_STATE), BF16),
            jax.ShapeDtypeStruct((m, D_GLA_V), F32),
        ],
        scratch_shapes=_MIX_SCRATCH + [
            pltpu.VMEM((D_SSD // LANES, q + 2 * HALO, LANES), F32),
            pltpu.VMEM((2 * SSD_GROUPS * SSD_STATE // LANES, q + 2 * HALO, LANES), F32),
        ],
        compiler_params=_cparams(("arbitrary",)),
        name="mix_fwd",
    )(*([proj] * 6), small, *([proj] * 3), *params)


def _mix_bwd(proj, small, y_f, xs_c, bc_c, o_f, prm, cps):
    m = proj.shape[0]
    q = MIX_CHUNKS_BWD * SSD_CHUNK
    nc = m // q

    def main(colblk):
        return lambda s: (nc - 1 - s, colblk)

    in_specs = [
        pl.BlockSpec((q, D_SSD), main(0)),
        pl.BlockSpec((q, 2 * SSD_GROUPS * SSD_STATE), main(0)),
        pl.BlockSpec((q, SMALL_W), main(0)),
        pl.BlockSpec((q, 2048), main(COL_Z // 2048)),
        pl.BlockSpec((q, D_SSD), main(0)),
        pl.BlockSpec((q, D_GLA_K), main(COL_Q // D_GLA_K)),
        pl.BlockSpec((q, D_GLA_K), main(COL_K // D_GLA_K)),
        pl.BlockSpec((q, D_GLA_V), main(COL_V // D_GLA_V)),
        pl.BlockSpec((q, D_GLA_V), main(COL_GOUT // D_GLA_V)),
        pl.BlockSpec((q, D_GLA_V), main(0)),
    ]
    params = [prm["dtb_b"], prm["alog_b"], prm["expand"], prm["dskip_x"], prm["ssd_nw"],
              prm["gate_w_b"], prm["gate_b_b"], prm["gla_nw"]]
    in_specs += [_const_spec(a) for a in params]
    return pl.pallas_call(
        functools.partial(_mix_bwd_kernel, cps=cps),
        grid=(nc,),
        in_specs=in_specs,
        out_specs=[pl.BlockSpec((q, D_SSD), main(0)), pl.BlockSpec((q, D_GLA_V), main(0))],
        out_shape=[jax.ShapeDtypeStruct((m, D_SSD), BF16), jax.ShapeDtypeStruct((m, D_GLA_V), BF16)],
        scratch_shapes=_MIX_SCRATCH,
        compiler_params=_cparams(("arbitrary",)),
        name="mix_bwd",
    )(xs_c, bc_c, small, proj, y_f, proj, proj, proj, proj, o_f, *params)


def _out_proj_kernel(x_ref, ya_ref, yb_ref, wa_ref, wb_ref, o_ref):
    o_ref[...] = x_ref[...] + _dot(ya_ref[...], wa_ref[...]) + _dot(yb_ref[...], wb_ref[...])


def _out_proj(x, ya, yb, wa, wb):
    m, d = x.shape
    ka = ya.shape[1]
    kb = yb.shape[1]
    tm, tn = TM_OUT, TN_OUT
    return pl.pallas_call(
        _out_proj_kernel,
        grid=(m // tm, d // tn),
        in_specs=[
            pl.BlockSpec((tm, tn), lambda i, j: (i, j)),
            pl.BlockSpec((tm, ka), lambda i, j: (i, 0)),
            pl.BlockSpec((tm, kb), lambda i, j: (i, 0)),
            pl.BlockSpec((ka, tn), lambda i, j: (0, j)),
            pl.BlockSpec((kb, tn), lambda i, j: (0, j)),
        ],
        out_specs=pl.BlockSpec((tm, tn), lambda i, j: (i, j)),
        out_shape=jax.ShapeDtypeStruct((m, d), F32),
        compiler_params=_cparams(("parallel", "arbitrary")),
        name="out_proj",
    )(x, ya, yb, wa, wb)


def _xattn_kernel(x_ref, nw_ref, wq_ref, kv_ref, wo_ref, o_ref):
    x = x_ref[...]
    h = _rms_normalize(x, nw_ref[...]).astype(BF16)
    qq = _dot(h, wq_ref[...]).astype(BF16)
    scale = XATTN_HEAD_DIM ** -0.5
    heads = []
    for hd in range(XATTN_HEADS):
        ds = slice(hd * XATTN_HEAD_DIM, (hd + 1) * XATTN_HEAD_DIM)
        kh = kv_ref[:, ds]
        vh = kv_ref[:, D_MODEL + hd * XATTN_HEAD_DIM:D_MODEL + (hd + 1) * XATTN_HEAD_DIM]
        s = _dot_nt(qq[:, ds], kh) * scale
        s = s - jnp.max(s, axis=-1, keepdims=True)
        e = jnp.exp(s)
        p = e / jnp.sum(e, axis=-1, keepdims=True)
        heads.append(_dot(p.astype(BF16), vh).astype(BF16))
    o = jnp.concatenate(heads, axis=1)
    o_ref[...] = x + _dot(o, wo_ref[...])


def _xattn(x, nw, wq, kv, wo, tiles_per_seq):
    m, d = x.shape
    tm = TM_XATTN
    single = pl.Buffered(1)
    return pl.pallas_call(
        _xattn_kernel,
        grid=(m // tm,),
        in_specs=[
            pl.BlockSpec((tm, d), lambda i: (i, 0)),
            pl.BlockSpec((1, d), lambda i: (0, 0)),
            pl.BlockSpec((d, d), lambda i: (0, 0), pipeline_mode=single),
            pl.BlockSpec((N_MEM, 2 * d), lambda i: (i // tiles_per_seq, 0)),
            pl.BlockSpec((d, d), lambda i: (0, 0), pipeline_mode=single),
        ],
        out_specs=pl.BlockSpec((tm, d), lambda i: (i, 0)),
        out_shape=jax.ShapeDtypeStruct((m, d), F32),
        compiler_params=_cparams(("arbitrary",)),
        name="xattn",
    )(x, nw, wq, kv, wo)


_W_IN_SEGMENTS = (
    (COL_Z, 0, 4096),
    (COL_V, 7232, 2048),
    (COL_GOUT, 9312, 2048),
    (COL_BC, 4096, 1024),
    (COL_Q, 5184, 1024),
    (COL_K, 6208, 1024),
)
_W_IN_DT = (5120, 5184)
_W_IN_LOW = (9280, 9312)
W_RELAYOUT_ROWS = 256


def _w_in_relayout_kernel(w_ref, main_ref, small_ref):
    for dst, src, width in _W_IN_SEGMENTS:
        main_ref[:, dst:dst + width] = w_ref[:, src:src + width]
    rows = w_ref.shape[0]
    narrow = _W_IN_DT[1] - _W_IN_DT[0] + _W_IN_LOW[1] - _W_IN_LOW[0]
    small_ref[...] = jnp.concatenate(
        [w_ref[:, _W_IN_DT[0]:_W_IN_DT[1]], w_ref[:, _W_IN_LOW[0]:_W_IN_LOW[1]],
         jnp.zeros((rows, SMALL_W - narrow), w_ref.dtype)], axis=1)


def _relayout_w_in(w_in):
    d, n_in = w_in.shape
    tr = W_RELAYOUT_ROWS
    return pl.pallas_call(
        _w_in_relayout_kernel,
        grid=(d // tr,),
        in_specs=[pl.BlockSpec((tr, n_in), lambda i: (i, 0))],
        out_specs=[pl.BlockSpec((tr, N_PROJ), lambda i: (i, 0)), pl.BlockSpec((tr, SMALL_W), lambda i: (i, 0))],
        out_shape=[jax.ShapeDtypeStruct((d, N_PROJ), w_in.dtype), jax.ShapeDtypeStruct((d, SMALL_W), w_in.dtype)],
        compiler_params=_cparams(("arbitrary",)),
        name="w_in_relayout",
    )(w_in)


def _prepare(p):
    w_main, w_small = _relayout_w_in(p["w_in"].astype(BF16))

    def gate_w(w, row0):
        full = jnp.zeros((SMALL_W, D_GLA_K), F32)
        return full.at[row0:row0 + GLA_RANK].set(w).astype(BF16)

    head_of_lane = jnp.arange(D_SSD) // SSD_HEADDIM
    rows = jnp.arange(4 * SSD_HEADS)[:, None]
    expand = jnp.stack([(rows == (head_of_lane[None, :] + SSD_HEADS * t)) for t in range(3)], axis=0)
    conv_w, conv_b = p["conv_w"], p["conv_b"]
    return dict(
        w_main=w_main, w_small=w_small,
        cw_x=conv_w[:, :D_SSD], cb_x=conv_b[None, :D_SSD],
        cw_bc=conv_w[:, D_SSD:], cb_bc=conv_b[None, D_SSD:],
        dtb_f=p["dt_bias_fwd"][:, None], dtb_b=p["dt_bias_bwd"][:, None],
        alog_f=p["a_log_fwd"][:, None], alog_b=p["a_log_bwd"][:, None],
        expand=expand.astype(BF16),
        dskip_x=jnp.repeat(p["d_skip"], SSD_HEADDIM)[None, :],
        ssd_nw=p["ssd_norm"][None, :],
        gate_w_f=gate_w(p["gla_gate_w_fwd"], 2 * SSD_HEADS),
        gate_w_b=gate_w(p["gla_gate_w_bwd"], 2 * SSD_HEADS + GLA_RANK),
        gate_b_f=p["gla_gate_b_fwd"][None, :], gate_b_b=p["gla_gate_b_bwd"][None, :],
        gla_nw=p["gla_norm"][None, :],
        w_out_a=p["w_out"][:D_SSD].astype(BF16), w_out_b=p["w_out"][D_SSD:].astype(BF16),
        ffn1=(p["ffn1_norm"][None, :], p["ffn1_w1"].astype(BF16), p["ffn1_w3"].astype(BF16), p["ffn1_w2"].astype(BF16)),
        ffn2=(p["ffn2_norm"][None, :], p["ffn2_w1"].astype(BF16), p["ffn2_w3"].astype(BF16), p["ffn2_w2"].astype(BF16)),
        mix_nw=p["mix_norm"][None, :],
        xattn_nw=p["xattn_norm"][None, :], mem_nw=p["mem_norm"][None, :],
        w_cq=p["w_cq"].astype(BF16), w_ckv=p["w_ckv"].astype(BF16), w_co=p["w_co"].astype(BF16),
        final_nw=p["final_norm"][None, :],
    )


def _trunk(x3, mem3, prm):
    b, l, d = x3.shape
    x = x3.reshape(b * l, d)
    mem = mem3.reshape(b * N_MEM, d)
    x = _ffn(x, *prm["ffn1"])
    proj, small = _in_proj(x, prm["mix_nw"], prm["w_main"], prm["w_small"])
    y_f, xs_c, bc_c, o_f = _mix_fwd(proj, small, prm, l // (MIX_CHUNKS_FWD * SSD_CHUNK))
    y, o = _mix_bwd(proj, small, y_f, xs_c, bc_c, o_f, prm, l // (MIX_CHUNKS_BWD * SSD_CHUNK))
    x = _out_proj(x, y, o, prm["w_out_a"], prm["w_out_b"])
    kv = _norm_matmul(mem, prm["mem_nw"], prm["w_ckv"], BF16, N_MEM, 1024, "kv_proj")
    x = _xattn(x, prm["xattn_nw"], prm["w_cq"], kv, prm["w_co"], l // TM_XATTN)
    x = _ffn(x, *prm["ffn2"], final_nw=prm["final_nw"])
    return x.reshape(b, l, d)


def kernel(x_prompt, x_sample, mem_prompt, mem_sample, ffn1_norm, ffn1_w1, ffn1_w3, ffn1_w2, mix_norm, w_in, conv_w, conv_b, dt_bias_fwd, dt_bias_bwd, a_log_fwd, a_log_bwd, d_skip, ssd_norm, gla_gate_w_fwd, gla_gate_b_fwd, gla_gate_w_bwd, gla_gate_b_bwd, gla_norm, w_out, xattn_norm, mem_norm, w_cq, w_ckv, w_co, ffn2_norm, ffn2_w1, ffn2_w3, ffn2_w2, final_norm):
    p = dict(
        ffn1_norm=ffn1_norm[0], ffn1_w1=ffn1_w1[0], ffn1_w3=ffn1_w3[0], ffn1_w2=ffn1_w2[0],
        mix_norm=mix_norm[0], w_in=w_in[0], conv_w=conv_w[0], conv_b=conv_b[0],
        dt_bias_fwd=dt_bias_fwd[0], dt_bias_bwd=dt_bias_bwd[0], a_log_fwd=a_log_fwd[0], a_log_bwd=a_log_bwd[0],
        d_skip=d_skip[0], ssd_norm=ssd_norm[0],
        gla_gate_w_fwd=gla_gate_w_fwd[0], gla_gate_b_fwd=gla_gate_b_fwd[0],
        gla_gate_w_bwd=gla_gate_w_bwd[0], gla_gate_b_bwd=gla_gate_b_bwd[0],
        gla_norm=gla_norm[0], w_out=w_out[0], xattn_norm=xattn_norm[0], mem_norm=mem_norm[0],
        w_cq=w_cq[0], w_ckv=w_ckv[0], w_co=w_co[0],
        ffn2_norm=ffn2_norm[0], ffn2_w1=ffn2_w1[0], ffn2_w3=ffn2_w3[0], ffn2_w2=ffn2_w2[0],
        final_norm=final_norm,
    )
    prm = _prepare(p)
    return (_trunk(x_prompt, mem_prompt, prm), _trunk(x_sample, mem_sample, prm))
```
